```python
import math
import jax, jax.numpy as jnp
from jax import lax
import numpy as np

D_MODEL = 1024
BATCH = 8
SEQ = 4096
DEPTH = 2
DEC_BATCH = 32
DEC_SEQ = 1
PAST_LEN = 16384
PAGE_SIZE = 128

N_MIXERS = 2
N_ATT_LAYERS = (DEPTH + 1) // 2
N_SSM_LAYERS = DEPTH // 2
H_ATT = 16
HEAD_DIM = D_MODEL // H_ATT
MOBA_BLOCK = 256
MOBA_TOPK = 3
Q_CHUNK = 16
REL_BUCKETS = 32
REL_MAX_DIST = 128
SSM_GROUP = 16
SSM_GROUPS = D_MODEL // SSM_GROUP
SSM_STATE = 64
SSM_CHUNK = 128
D_FF = 2816
RMS_EPS = 1e-6
NEG = -1e30

kernel_name = "moba_s5_macaron_hybrid_step"


def rmsnorm(x, g):
    x32 = x.astype(jnp.float32)
    y = x32 * lax.rsqrt(jnp.mean(x32 * x32, axis=-1, keepdims=True) + RMS_EPS) * g.astype(jnp.float32)
    return y.astype(x.dtype)


def swiglu(x, w_in, w_out):
    gate, up = jnp.split(x @ w_in, 2, axis=-1)
    return (jax.nn.silu(gate) * up) @ w_out


def rel_bucket(n):
    n = jnp.maximum(n, 0)
    max_exact = REL_BUCKETS // 2
    nf = jnp.maximum(n, max_exact).astype(jnp.float32)
    large = max_exact + (jnp.log(nf / max_exact) / math.log(REL_MAX_DIST / max_exact)
                         * (REL_BUCKETS - max_exact)).astype(jnp.int32)
    large = jnp.minimum(large, REL_BUCKETS - 1)
    return jnp.where(n < max_exact, n, large)


def qkv_heads(xn, w_qkv):
    b, t, _ = xn.shape
    q, k, v = jnp.split(xn @ w_qkv, 3, axis=-1)
    return (q.reshape(b, t, H_ATT, HEAD_DIM), k.reshape(b, t, H_ATT, HEAD_DIM),
            v.reshape(b, t, H_ATT, HEAD_DIM))


def select_blocks(qh, kmean, n_full):
    nblk = kmean.shape[2]
    g = jnp.einsum('bhqd,bhnd->bhqn', qh.astype(jnp.float32), kmean)
    g = jnp.where(jnp.arange(nblk)[None, :] < n_full[:, None], g, NEG)
    if nblk < MOBA_TOPK:
        g = jnp.pad(g, ((0, 0), (0, 0), (0, 0), (0, MOBA_TOPK - nblk)), constant_values=NEG)
    _, idx = lax.top_k(g, MOBA_TOPK)
    idx = jnp.minimum(idx, nblk - 1)
    slot_ok = jnp.arange(MOBA_TOPK)[None, :] < n_full[:, None]
    return idx, slot_ok


def moba_prompt(xn, w_qkv, w_o, rel_t):
    b, t, _ = xn.shape
    q, k, v = qkv_heads(xn, w_qkv)
    nblk = -(-t // MOBA_BLOCK)
    pad = nblk * MOBA_BLOCK - t

    def to_blocks(a):
        a = jnp.pad(a, ((0, 0), (0, pad), (0, 0), (0, 0)))
        return a.reshape(b, nblk, MOBA_BLOCK, H_ATT, HEAD_DIM).transpose(0, 3, 1, 2, 4)

    kb, vb = to_blocks(k), to_blocks(v)
    kmean = kb.astype(jnp.float32).mean(axis=3)
    n_chunks = t // Q_CHUNK
    qc = q.reshape(b, n_chunks, Q_CHUNK, H_ATT, HEAD_DIM).transpose(1, 0, 3, 2, 4)
    bi = jnp.arange(b)[:, None, None, None]
    hi = jnp.arange(H_ATT)[None, :, None, None]
    offs = jnp.arange(MOBA_BLOCK)
    scale = HEAD_DIM ** -0.5
    nsel = MOBA_TOPK * MOBA_BLOCK

    def chunk_attend(args):
        c, qb = args
        pos = c * Q_CHUNK + jnp.arange(Q_CHUNK)
        n_full = pos // MOBA_BLOCK
        idx, slot_ok = select_blocks(qb, kmean, n_full)
        ks = kb[bi, hi, idx]
        vs = vb[bi, hi, idx]
        sel_pos = idx[..., None] * MOBA_BLOCK + offs
        s_sel = (jnp.einsum('bhqd,bhqkjd->bhqkj', qb, ks).astype(jnp.float32) * scale
                 + rel_t[hi[..., None], rel_bucket(pos[:, None, None] - sel_pos)])
        s_sel = jnp.where(slot_ok[..., None], s_sel, NEG)
        own = pos[0] // MOBA_BLOCK
        ko = lax.dynamic_index_in_dim(kb, own, axis=2, keepdims=False)
        vo = lax.dynamic_index_in_dim(vb, own, axis=2, keepdims=False)
        own_pos = own * MOBA_BLOCK + offs
        s_own = (jnp.einsum('bhqd,bhjd->bhqj', qb, ko).astype(jnp.float32) * scale
                 + rel_t[:, rel_bucket(pos[:, None] - own_pos[None, :])])
        s_own = jnp.where(own_pos[None, :] <= pos[:, None], s_own, NEG)
        logits = jnp.concatenate([s_sel.reshape(b, H_ATT, Q_CHUNK, nsel), s_own], axis=-1)
        p = jax.nn.softmax(logits, axis=-1).astype(v.dtype)
        p_sel = p[..., :nsel].reshape(b, H_ATT, Q_CHUNK, MOBA_TOPK, MOBA_BLOCK)
        return (jnp.einsum('bhqkj,bhqkjd->bhqd', p_sel, vs)
                + jnp.einsum('bhqj,bhjd->bhqd', p[..., nsel:], vo))

    o = lax.map(chunk_attend, (jnp.arange(n_chunks), qc))
    o = o.transpose(1, 0, 3, 2, 4).reshape(b, t, D_MODEL)
    return o @ w_o, k, v


def moba_sample(xn, cache_k, cache_v, page_table, w_qkv, w_o, rel_t):
    b, t, _ = xn.shape
    q, k, v = qkv_heads(xn, w_qkv)
    n_pages = PAST_LEN // PAGE_SIZE
    nblk = -(-(PAST_LEN + t) // MOBA_BLOCK)
    pos = PAST_LEN + jnp.arange(t)
    page_sum = cache_k.astype(jnp.float32).sum(axis=1)[page_table]
    blk_sum = (jax.ops.segment_sum(page_sum.transpose(1, 0, 2, 3),
                                   jnp.arange(n_pages) * PAGE_SIZE // MOBA_BLOCK, num_segments=nblk)
               + jax.ops.segment_sum(k.astype(jnp.float32).transpose(1, 0, 2, 3),
                                     pos // MOBA_BLOCK, num_segments=nblk))
    kmean = blk_sum.transpose(1, 2, 0, 3) / MOBA_BLOCK
    qh = q.transpose(0, 2, 1, 3)
    n_full = pos // MOBA_BLOCK
    idx, slot_ok = select_blocks(qh, kmean, n_full)
    blocks = jnp.concatenate([idx, jnp.broadcast_to(n_full[:, None], (b, H_ATT, t, 1))], axis=-1)
    row_pos = blocks[..., None] * MOBA_BLOCK + jnp.arange(MOBA_BLOCK)
    bi5 = jnp.arange(b)[:, None, None, None, None]
    hi5 = jnp.arange(H_ATT)[None, :, None, None, None]
    in_cache = (row_pos < PAST_LEN)[..., None]
    phys = page_table[bi5, jnp.minimum(row_pos // PAGE_SIZE, n_pages - 1)]
    off = row_pos % PAGE_SIZE
    new_i = jnp.clip(row_pos - PAST_LEN, 0, t - 1)
    kk = jnp.where(in_cache, cache_k[phys, off, hi5], k[bi5, new_i, hi5])
    vv = jnp.where(in_cache, cache_v[phys, off, hi5], v[bi5, new_i, hi5])
    ok = (jnp.concatenate([slot_ok, jnp.ones((t, 1), bool)], axis=-1)[..., None]
          & (row_pos <= pos[:, None, None]))
    s = (jnp.einsum('bhtd,bhtkjd->bhtkj', qh, kk).astype(jnp.float32) * HEAD_DIM ** -0.5
         + rel_t[hi5, rel_bucket(pos[:, None, None] - row_pos)])
    s = jnp.where(ok, s, NEG)
    p = jax.nn.softmax(s.reshape(b, H_ATT, t, -1), axis=-1).reshape(s.shape).astype(v.dtype)
    o = jnp.einsum('bhtkj,bhtkjd->bhtd', p, vv).transpose(0, 2, 1, 3).reshape(b, t, D_MODEL)
    return o @ w_o, k, v


def _combine(e1, e2):
    a1r, a1i, b1r, b1i = e1
    a2r, a2i, b2r, b2i = e2
    return (a2r * a1r - a2i * a1i, a2r * a1i + a2i * a1r,
            a2r * b1r - a2i * b1i + b2r, a2r * b1i + a2i * b1r + b2i)


def s5_mix(xn, h_re, h_im, chunk, log_dt, a_re, a_im, b_re, b_im, c_re, c_im, d_skip, w_glu, b_glu):
    b, t, _ = xn.shape
    f32 = jnp.float32
    dt = jnp.exp(log_dt.astype(f32))[:, None]
    ar, ai = a_re.astype(f32), a_im.astype(f32)
    mag = jnp.exp(dt * ar)
    abar_re, abar_im = mag * jnp.cos(dt * ai), mag * jnp.sin(dt * ai)
    den = ar * ar + ai * ai
    coef_re = ((abar_re - 1.0) * ar + abar_im * ai) / den
    coef_im = (abar_im * ar - (abar_re - 1.0) * ai) / den
    br, bim = b_re.astype(f32), b_im.astype(f32)
    bbar_re = coef_re[..., None] * br - coef_im[..., None] * bim
    bbar_im = coef_re[..., None] * bim + coef_im[..., None] * br
    cr, ci = c_re.astype(f32), c_im.astype(f32)
    u = xn.astype(f32).reshape(b, t // chunk, chunk, SSM_GROUPS, SSM_GROUP).transpose(1, 0, 2, 3, 4)

    def step(carry, uc):
        hr, hi = carry
        bu_re = jnp.einsum('bcgi,gpi->bcgp', uc, bbar_re)
        bu_im = jnp.einsum('bcgi,gpi->bcgp', uc, bbar_im)
        bu_re = bu_re.at[:, 0].add(abar_re * hr - abar_im * hi)
        bu_im = bu_im.at[:, 0].add(abar_re * hi + abar_im * hr)
        a_r = jnp.broadcast_to(abar_re, bu_re.shape)
        a_i = jnp.broadcast_to(abar_im, bu_im.shape)
        _, _, sr, si = lax.associative_scan(_combine, (a_r, a_i, bu_re, bu_im), axis=1)
        y = jnp.einsum('bcgp,gip->bcgi', sr, cr) - jnp.einsum('bcgp,gip->bcgi', si, ci)
        return (sr[:, -1], si[:, -1]), y

    (hr, hi), y = lax.scan(step, (h_re.astype(f32), h_im.astype(f32)), u)
    y = y.transpose(1, 0, 2, 3, 4).reshape(b, t, D_MODEL) + d_skip.astype(f32) * xn.astype(f32)
    z = jax.nn.gelu(y).astype(xn.dtype)
    ga, gb = jnp.split(z @ w_glu + b_glu, 2, axis=-1)
    return ga * jax.nn.sigmoid(gb), hr, hi


def setup_inputs(seed: int = 0) -> dict:
    key = jax.random.key(seed)
    ks = jax.random.split(key, 24)
    n_pages = PAST_LEN // PAGE_SIZE
    n_phys = (DEC_BATCH * n_pages * 5) // 4
    nrm = jax.random.normal
    f32 = jnp.float32
    page_table = jax.random.permutation(ks[6], n_phys)[:DEC_BATCH * n_pages].reshape(DEC_BATCH, n_pages).astype(jnp.int32)
    a_im = jnp.broadcast_to(math.pi * jnp.arange(SSM_STATE, dtype=f32), (N_SSM_LAYERS, SSM_GROUPS, SSM_STATE))
    return {
        "x_prompt": nrm(ks[0], (BATCH, SEQ, D_MODEL), f32),
        "x_sample": nrm(ks[1], (DEC_BATCH, DEC_SEQ, D_MODEL), f32),
        "cache_k": nrm(ks[2], (N_ATT_LAYERS, n_phys, PAGE_SIZE, H_ATT, HEAD_DIM), f32),
        "cache_v": nrm(ks[3], (N_ATT_LAYERS, n_phys, PAGE_SIZE, H_ATT, HEAD_DIM), f32),
        "state_ssm_re": 0.5 * nrm(ks[4], (N_SSM_LAYERS, DEC_BATCH, SSM_GROUPS, SSM_STATE), f32),
        "state_ssm_im": 0.5 * nrm(ks[5], (N_SSM_LAYERS, DEC_BATCH, SSM_GROUPS, SSM_STATE), f32),
        "page_table": page_table,
        "rel_bias": 0.5 * nrm(ks[7], (REL_BUCKETS, H_ATT), f32),
        "norm_g": 1.0 + 0.02 * nrm(ks[8], (DEPTH, 3, D_MODEL), f32),
        "final_norm_g": 1.0 + 0.02 * nrm(ks[9], (D_MODEL,), f32),
        "ffn_w_in": nrm(ks[10], (DEPTH, 2, D_MODEL, 2 * D_FF), f32) * D_MODEL ** -0.5,
        "ffn_w_out": nrm(ks[11], (DEPTH, 2, D_FF, D_MODEL), f32) * D_FF ** -0.5,
        "att_w_qkv": nrm(ks[12], (N_ATT_LAYERS, D_MODEL, 3 * D_MODEL), f32) * D_MODEL ** -0.5,
        "att_w_o": nrm(ks[13], (N_ATT_LAYERS, D_MODEL, D_MODEL), f32) * D_MODEL ** -0.5,
        "ssm_log_dt": jax.random.uniform(ks[14], (N_SSM_LAYERS, SSM_GROUPS), f32, math.log(1e-3), math.log(1e-1)),
        "ssm_a_re": -0.5 + 0.01 * nrm(ks[15], (N_SSM_LAYERS, SSM_GROUPS, SSM_STATE), f32),
        "ssm_a_im": a_im,
        "ssm_b_re": nrm(ks[16], (N_SSM_LAYERS, SSM_GROUPS, SSM_STATE, SSM_GROUP), f32) * (2 * SSM_GROUP) ** -0.5,
        "ssm_b_im": nrm(ks[17], (N_SSM_LAYERS, SSM_GROUPS, SSM_STATE, SSM_GROUP), f32) * (2 * SSM_GROUP) ** -0.5,
        "ssm_c_re": nrm(ks[18], (N_SSM_LAYERS, SSM_GROUPS, SSM_GROUP, SSM_STATE), f32) * SSM_STATE ** -0.5,
        "ssm_c_im": nrm(ks[19], (N_SSM_LAYERS, SSM_GROUPS, SSM_GROUP, SSM_STATE), f32) * SSM_STATE ** -0.5,
        "ssm_d": nrm(ks[20], (N_SSM_LAYERS, D_MODEL), f32),
        "ssm_w_glu": nrm(ks[21], (N_SSM_LAYERS, D_MODEL, 2 * D_MODEL), f32) * D_MODEL ** -0.5,
        "ssm_b_glu": 0.01 * nrm(ks[22], (N_SSM_LAYERS, 2 * D_MODEL), f32),
    }


def reference(x_prompt, x_sample, cache_k, cache_v, state_ssm_re, state_ssm_im, page_table,
              rel_bias, norm_g, final_norm_g, ffn_w_in, ffn_w_out, att_w_qkv, att_w_o,
              ssm_log_dt, ssm_a_re, ssm_a_im, ssm_b_re, ssm_b_im, ssm_c_re, ssm_c_im,
              ssm_d, ssm_w_glu, ssm_b_glu):
    rel_t = rel_bias.T
    xp, xs = x_prompt, x_sample
    k_p, v_p, k_s, v_s = [], [], [], []
    hr_p, hi_p, hr_s, hi_s = [], [], [], []
    for i in range(DEPTH):
        j = i // N_MIXERS
        xp = xp + 0.5 * swiglu(rmsnorm(xp, norm_g[i, 0]), ffn_w_in[i, 0], ffn_w_out[i, 0])
        xs = xs + 0.5 * swiglu(rmsnorm(xs, norm_g[i, 0]), ffn_w_in[i, 0], ffn_w_out[i, 0])
        np_, ns_ = rmsnorm(xp, norm_g[i, 1]), rmsnorm(xs, norm_g[i, 1])
        if i % N_MIXERS == 0:
            mp, kp, vp = moba_prompt(np_, att_w_qkv[j], att_w_o[j], rel_t)
            ms, ksm, vsm = moba_sample(ns_, cache_k[j], cache_v[j], page_table, att_w_qkv[j], att_w_o[j], rel_t)
            k_p.append(kp); v_p.append(vp); k_s.append(ksm); v_s.append(vsm)
        else:
            prm = (ssm_log_dt[j], ssm_a_re[j], ssm_a_im[j], ssm_b_re[j], ssm_b_im[j],
                   ssm_c_re[j], ssm_c_im[j], ssm_d[j], ssm_w_glu[j], ssm_b_glu[j])
            h0 = jnp.zeros((xp.shape[0], SSM_GROUPS, SSM_STATE), jnp.float32)
            mp, hrp, hip = s5_mix(np_, h0, h0, min(SSM_CHUNK, xp.shape[1]), *prm)
            ms, hrs, his = s5_mix(ns_, state_ssm_re[j], state_ssm_im[j], xs.shape[1], *prm)
            hr_p.append(hrp); hi_p.append(hip); hr_s.append(hrs); hi_s.append(his)
        xp = xp + mp
        xs = xs + ms
        xp = xp + 0.5 * swiglu(rmsnorm(xp, norm_g[i, 2]), ffn_w_in[i, 1], ffn_w_out[i, 1])
        xs = xs + 0.5 * swiglu(rmsnorm(xs, norm_g[i, 2]), ffn_w_in[i, 1], ffn_w_out[i, 1])
    y_prompt = rmsnorm(xp, final_norm_g)
    y_sample = rmsnorm(xs, final_norm_g)
    return (y_prompt, y_sample, jnp.stack(k_p), jnp.stack(v_p), jnp.stack(k_s), jnp.stack(v_s),
            jnp.stack(hr_p), jnp.stack(hi_p), jnp.stack(hr_s), jnp.stack(hi_s))
```

```python
import functools
import math

import numpy as np
import jax
import jax.numpy as jnp
from jax import lax
from jax.experimental import pallas as pl
from jax.experimental.pallas import tpu as pltpu

F32 = jnp.float32
BF16 = jnp.bfloat16
I32 = jnp.int32

RMS_EPS = 1e-6
NEG = -1e30
MOBA_BLOCK = 256
MOBA_TOPK = 3
REL_MAX_DIST = 128
LANES = 128
SUBLANES = 8
VMEM_CAP = 60 * 1024 * 1024

_NT = (((1,), (1,)), ((), ()))


def _vmem_limit(nbytes):
    return int(min(VMEM_CAP, nbytes + (8 << 20)))


def _const_spec(shape):
    nd = len(shape)
    return pl.BlockSpec(shape, lambda *_: (0,) * nd, pipeline_mode=pl.Buffered(1))


def _rms(x, g):
    return x * lax.rsqrt(jnp.mean(x * x, axis=-1, keepdims=True) + RMS_EPS) * g


def _split3(x):
    a = x.astype(BF16)
    r = x - a.astype(F32)
    b = r.astype(BF16)
    c = (r - b.astype(F32)).astype(BF16)
    return a, b, c


def _ffn_kernel(*refs, d_ff, chunk, has_pre, has_final):
    it = iter(refs)
    x_ref = next(it)
    if has_pre:
        a_ref, wpre_ref = next(it), next(it)
    g_ref, win_ref, wout_ref = next(it), next(it), next(it)
    if has_final:
        gf_ref = next(it)
    o_ref, act_ref = next(it), next(it)

    x = x_ref[...]
    if has_pre:
        x = x + jnp.dot(a_ref[...], wpre_ref[...], preferred_element_type=F32)
    xn = _rms(x, g_ref[...]).astype(BF16)
    for c in range(d_ff // chunk):
        lo = c * chunk
        gate = jnp.dot(xn, win_ref[:, lo:lo + chunk], preferred_element_type=F32)
        up = jnp.dot(xn, win_ref[:, d_ff + lo:d_ff + lo + chunk], preferred_element_type=F32)
        act_ref[:, lo:lo + chunk] = (gate * jax.nn.sigmoid(gate) * up).astype(BF16)
    y = x + 0.5 * jnp.dot(act_ref[...], wout_ref[...], preferred_element_type=F32)
    if has_final:
        y = _rms(y, gf_ref[...])
    o_ref[...] = y


def _ffn(x, g, w_in, w_out, *, pre=None, final_g=None, name):
    n, d = x.shape
    d_ff = w_out.shape[0]
    tm = min(n, 512)
    chunk = 256
    assert n % tm == 0 and d_ff % chunk == 0
    row = lambda i: (i, 0)
    args, specs = [x], [pl.BlockSpec((tm, d), row)]
    if pre is not None:
        a, w_pre = pre
        args += [a, w_pre]
        specs += [pl.BlockSpec((tm, a.shape[1]), row), _const_spec(w_pre.shape)]
    args += [g.reshape(1, d), w_in, w_out]
    specs += [_const_spec((1, d)), _const_spec(w_in.shape), _const_spec(w_out.shape)]
    if final_g is not None:
        args.append(final_g.reshape(1, d))
        specs.append(_const_spec((1, d)))
    wbytes = 2 * (w_in.size + w_out.size + (pre[1].size if pre is not None else 0))
    need = wbytes + tm * d * 4 * 4 + tm * d_ff * 2 + tm * chunk * 4 * 6 + tm * d * 4 * 3
    return pl.pallas_call(
        functools.partial(_ffn_kernel, d_ff=d_ff, chunk=chunk, has_pre=pre is not None,
                          has_final=final_g is not None),
        grid=(n // tm,),
        in_specs=specs,
        out_specs=pl.BlockSpec((tm, d), row),
        out_shape=jax.ShapeDtypeStruct((n, d), F32),
        scratch_shapes=[pltpu.VMEM((tm, d_ff), BF16)],
        compiler_params=pltpu.CompilerParams(dimension_semantics=("parallel",),
                                             vmem_limit_bytes=_vmem_limit(need)),
        name=name,
    )(*args)


def _qkv_prompt_kernel(x_ref, g_ref, w_ref, k_ref, v_ref, qb_ref, kb_ref, vb_ref, km_ref, *, d, scale):
    xn = _rms(x_ref[...], g_ref[...]).astype(BF16)
    q = jnp.dot(xn, w_ref[:, 0:d], preferred_element_type=F32)
    qb_ref[...] = (q * scale).astype(BF16)
    k = jnp.dot(xn, w_ref[:, d:2 * d], preferred_element_type=F32)
    k_ref[...] = k
    kb_ref[...] = k.astype(BF16)
    for r in range(k.shape[0] // MOBA_BLOCK):
        km_ref[r] = jnp.sum(k[r * MOBA_BLOCK:(r + 1) * MOBA_BLOCK], axis=0, keepdims=True) * (1.0 / MOBA_BLOCK)
    v = jnp.dot(xn, w_ref[:, 2 * d:3 * d], preferred_element_type=F32)
    v_ref[...] = v
    vb_ref[...] = v.astype(BF16)


def _qkv_prompt(x, g, w_qkv, scale):
    n, d = x.shape
    tm = 512
    assert n % tm == 0 and tm % MOBA_BLOCK == 0
    row = lambda i: (i, 0)
    rs = pl.BlockSpec((tm, d), row)
    need = 2 * w_qkv.size + tm * d * (4 * 2 * 3 + 2 * 2 * 3) + tm * d * 4 * 4
    return pl.pallas_call(
        functools.partial(_qkv_prompt_kernel, d=d, scale=scale),
        grid=(n // tm,),
        in_specs=[rs, _const_spec((1, d)), _const_spec(w_qkv.shape)],
        out_specs=[rs, rs, rs, rs, rs,
                   pl.BlockSpec((tm // MOBA_BLOCK, 1, d), lambda i: (i, 0, 0))],
        out_shape=[jax.ShapeDtypeStruct((n, d), F32), jax.ShapeDtypeStruct((n, d), F32),
                   jax.ShapeDtypeStruct((n, d), BF16), jax.ShapeDtypeStruct((n, d), BF16),
                   jax.ShapeDtypeStruct((n, d), BF16),
                   jax.ShapeDtypeStruct((n // MOBA_BLOCK, 1, d), F32)],
        compiler_params=pltpu.CompilerParams(dimension_semantics=("parallel",),
                                             vmem_limit_bytes=_vmem_limit(need)),
        name="qkv_prompt",
    )(x, g.reshape(1, d), w_qkv)


def _qkv_sample_kernel(x_ref, g_ref, w_ref, q_ref, k_ref, v_ref, *, d):
    xn = _rms(x_ref[...], g_ref[...]).astype(BF16)
    q_ref[...] = jnp.dot(xn, w_ref[:, 0:d], preferred_element_type=F32)
    k_ref[...] = jnp.dot(xn, w_ref[:, d:2 * d], preferred_element_type=F32)
    v_ref[...] = jnp.dot(xn, w_ref[:, 2 * d:3 * d], preferred_element_type=F32)


def _qkv_sample(x, g, w_qkv):
    n, d = x.shape
    full = pl.BlockSpec((n, d), lambda i: (0, 0))
    sd = jax.ShapeDtypeStruct((n, d), F32)
    return pl.pallas_call(
        functools.partial(_qkv_sample_kernel, d=d),
        grid=(1,),
        in_specs=[full, _const_spec((1, d)), _const_spec(w_qkv.shape)],
        out_specs=[full, full, full],
        out_shape=[sd, sd, sd],
        compiler_params=pltpu.CompilerParams(vmem_limit_bytes=_vmem_limit(2 * w_qkv.size + 16 * n * d)),
        name="qkv_sample",
    )(x, g.reshape(1, d), w_qkv)


def _rel_bucket_np(n, n_buckets):
    n = np.maximum(n, 0)
    max_exact = n_buckets // 2
    nf = np.maximum(n, max_exact).astype(np.float32)
    large = max_exact + (np.log(nf / np.float32(max_exact)) / np.float32(math.log(REL_MAX_DIST / max_exact))
                         * np.float32(n_buckets - max_exact)).astype(np.int32)
    large = np.minimum(large, n_buckets - 1)
    return np.where(n < max_exact, n, large).astype(np.int32)


def _bias_kernel(rel_ref, bkt_ref, dbk_ref, tile_ref, dec_ref, *, n_buckets):
    h = pl.program_id(0)
    for t in range(2):
        b = bkt_ref[t]
        acc = jnp.full(b.shape, NEG, F32)
        for u in range(n_buckets):
            acc = jnp.where(b == u, rel_ref[u, h], acc)
        tile_ref[0, t] = acc
    b = dbk_ref[...]
    acc = jnp.zeros(b.shape, F32)
    for u in range(n_buckets):
        acc = jnp.where(b == u, rel_ref[u, h], acc)
    dec_ref[0] = acc


def _bias_tables(rel_bias, n_heads):
    n_buckets = rel_bias.shape[0]
    blk = MOBA_BLOCK
    qi = np.arange(blk)[:, None]
    ki = np.arange(blk)[None, :]
    own = np.where(ki <= qi, _rel_bucket_np(qi - ki, n_buckets), -1)
    adj = _rel_bucket_np(blk + qi - ki, n_buckets)
    bkt = np.stack([own, adj]).astype(np.int32)
    dbk = np.zeros((SUBLANES, blk), np.int32)
    dbk[0] = _rel_bucket_np(blk - np.arange(blk), n_buckets)
    dbk[1] = _rel_bucket_np(np.full((blk,), 2 * blk), n_buckets)
    dbk[2] = 0
    return pl.pallas_call(
        functools.partial(_bias_kernel, n_buckets=n_buckets),
        grid=(n_heads,),
        in_specs=[pl.BlockSpec(memory_space=pltpu.SMEM),
                  pl.BlockSpec((2, blk, blk), lambda h: (0, 0, 0)),
                  pl.BlockSpec((SUBLANES, blk), lambda h: (0, 0))],
        out_specs=[pl.BlockSpec((1, 2, blk, blk), lambda h: (h, 0, 0, 0)),
                   pl.BlockSpec((1, SUBLANES, blk), lambda h: (h, 0, 0))],
        out_shape=[jax.ShapeDtypeStruct((n_heads, 2, blk, blk), F32),
                   jax.ShapeDtypeStruct((n_heads, SUBLANES, blk), F32)],
        name="rel_bias_tables",
    )(rel_bias, jnp.asarray(bkt), jnp.asarray(dbk))


def _moba_kernel(rel_ref, q_ref, k_ref, v_ref, km_ref, tile_ref, o_ref, s_ref, m_ref, acc_ref, *,
                 n_blk, hd, far_bucket):
    blk = MOBA_BLOCK
    hp = pl.program_id(1)
    i = pl.program_id(2)
    q = q_ref[0]
    lane = lax.broadcasted_iota(I32, (blk, LANES), 1)
    km = km_ref[0]
    km_hi = km.astype(BF16)
    km_lo = (km - km_hi.astype(F32)).astype(BF16)
    bidx = lax.broadcasted_iota(I32, (n_blk, blk), 0)
    valid = bidx < i
    ones_col = jnp.where(lane == 0, 1.0, 0.0).astype(BF16)
    ja = jnp.maximum(i - 1, 0)
    outs = []
    for hh in range(2):
        h = 2 * hp + hh
        qm = jnp.where((lane >= hh * hd) & (lane < (hh + 1) * hd), q, jnp.zeros_like(q))
        gt = (lax.dot_general(km_hi, qm, _NT, preferred_element_type=F32)
              + lax.dot_general(km_lo, qm, _NT, preferred_element_type=F32))
        gm = jnp.where(valid, gt, NEG)
        rank = jnp.zeros((n_blk, blk), I32)
        for jp in range(n_blk):
            row = gm[jp:jp + 1, :]
            ahead = jnp.where(row > gm, 1, jnp.where(row == gm, jnp.where(bidx > jp, 1, 0), 0))
            rank = rank + ahead
        far = rel_ref[far_bucket, h]
        chosen = jnp.where(rank < MOBA_TOPK, jnp.where(bidx <= i - 2, far, 0.0), NEG)
        aval = jnp.where(valid, chosen, NEG)
        a_hi = aval.astype(BF16).astype(F32)
        a_lo = (aval - a_hi).astype(BF16).astype(F32)
        at = jnp.concatenate([a_hi, a_lo, jnp.zeros((LANES - 2 * n_blk, blk), F32)], axis=0)
        a = at.T.astype(BF16)
        q_aug = jnp.concatenate([qm, a], axis=1)

        def k_aug(j):
            kj = k_ref[0, pl.ds(pl.multiple_of(j * blk, blk), blk), :]
            ej = jnp.where((lane == j) | (lane == n_blk + j), 1.0, 0.0).astype(BF16)
            return jnp.concatenate([kj, ej], axis=1)

        def v_aug(j):
            vj = v_ref[0, pl.ds(pl.multiple_of(j * blk, blk), blk), :]
            return jnp.concatenate([vj, ones_col], axis=1)

        k_own = k_ref[0, pl.ds(pl.multiple_of(i * blk, blk), blk), :]
        s_own = lax.dot_general(qm, k_own, _NT, preferred_element_type=F32) + tile_ref[hh, 0]
        s_adj = lax.dot_general(q_aug, k_aug(ja), _NT, preferred_element_type=F32) + tile_ref[hh, 1]
        s_ref[0] = s_own
        s_ref[1] = s_adj
        m_ref[...] = jnp.maximum(s_own, s_adj)

        def pass_a(j, c):
            s = lax.dot_general(q_aug, k_aug(j), _NT, preferred_element_type=F32)
            s_ref[2 + j] = s
            m_ref[...] = jnp.maximum(m_ref[...], s)
            return c

        lax.fori_loop(0, i - 1, pass_a, 0)
        m = jnp.max(m_ref[...], axis=-1, keepdims=True)

        acc_ref[...] = (jnp.dot(jnp.exp(s_ref[0] - m).astype(BF16), v_aug(i), preferred_element_type=F32)
                        + jnp.dot(jnp.exp(s_ref[1] - m).astype(BF16), v_aug(ja), preferred_element_type=F32))

        def pass_b(j, c):
            p = jnp.exp(s_ref[2 + j] - m).astype(BF16)
            acc_ref[...] += jnp.dot(p, v_aug(j), preferred_element_type=F32)
            return c

        lax.fori_loop(0, i - 1, pass_b, 0)
        acc = acc_ref[...]
        outs.append(acc[:, 0:LANES] / acc[:, LANES:LANES + 1])
    o_ref[0] = jnp.where(lane < hd, outs[0], outs[1]).astype(BF16)


def _moba_prompt(qb, kb, vb, km, tiles, rel_bias, *, n_heads):
    b, t, d = qb.shape
    hd = d // n_heads
    assert 2 * hd == LANES and t % MOBA_BLOCK == 0
    n_blk = t // MOBA_BLOCK
    assert 2 * n_blk <= LANES
    blk = MOBA_BLOCK
    far_bucket = int(_rel_bucket_np(np.array([2 * blk]), rel_bias.shape[0])[0])
    assert int(_rel_bucket_np(np.array([blk + 1]), rel_bias.shape[0])[0]) == far_bucket
    seq = pl.BlockSpec((1, t, LANES), lambda bi, hp, i: (bi, 0, hp))
    qblk = pl.BlockSpec((1, blk, LANES), lambda bi, hp, i: (bi, i, hp))
    need = 2 * 2 * t * LANES * 2 + (n_blk + 2) * blk * blk * 4 + 2 * blk * blk * 4 + 2 * 4 * blk * blk * 4 + (8 << 20)
    return pl.pallas_call(
        functools.partial(_moba_kernel, n_blk=n_blk, hd=hd, far_bucket=far_bucket),
        grid=(b, n_heads // 2, n_blk),
        in_specs=[pl.BlockSpec(memory_space=pltpu.SMEM), qblk, seq, seq,
                  pl.BlockSpec((1, n_blk, LANES), lambda bi, hp, i: (bi, 0, hp)),
                  pl.BlockSpec((2, 2, blk, blk), lambda bi, hp, i: (hp, 0, 0, 0))],
        out_specs=qblk,
        out_shape=jax.ShapeDtypeStruct((b, t, d), BF16),
        scratch_shapes=[pltpu.VMEM((n_blk + 2, blk, blk), F32), pltpu.VMEM((blk, blk), F32),
                        pltpu.VMEM((blk, 2 * LANES), F32)],
        compiler_params=pltpu.CompilerParams(dimension_semantics=("parallel", "parallel", "arbitrary"),
                                             vmem_limit_bytes=_vmem_limit(need)),
        name="moba_prompt",
    )(rel_bias, qb, kb, vb, km, tiles)


def _decode_gate_kernel(pt_ref, *refs, n_full, ppb, bps, n_heads, hd):
    n_pg = bps * ppb
    pages = refs[0:n_pg]
    q_ref, idx_ref, qb_ref, g_ref = refs[n_pg:]
    jj = pl.program_id(1)
    lane = lax.broadcasted_iota(I32, (n_heads, LANES), 1)

    @pl.when(jj == 0)
    def _():
        q128 = jnp.broadcast_to(q_ref[0], (LANES, n_heads * hd))
        for c in range(n_heads * hd // LANES):
            qb_ref[c * LANES:(c + 1) * LANES, :] = q128[:, c * LANES:(c + 1) * LANES].T
        g_ref[...] = jnp.zeros_like(g_ref)

    qb = qb_ref[...]
    for bb in range(bps):
        ksum = pages[bb * ppb][0].reshape(n_heads * hd, LANES)
        for r in range(1, ppb):
            ksum = ksum + pages[bb * ppb + r][0].reshape(n_heads * hd, LANES)
        prod = (ksum * qb).reshape(n_heads, hd, LANES)
        tot = jnp.sum(jnp.sum(prod, axis=1), axis=-1, keepdims=True) * (1.0 / MOBA_BLOCK)
        g_ref[...] = jnp.where(lane == jj * bps + bb, tot, g_ref[...])

    @pl.when(jj == pl.num_programs(1) - 1)
    def _():
        g = jnp.where(lane < n_full, g_ref[...], -3e38)
        out = jnp.zeros((n_heads, LANES), I32)
        for s in range(MOBA_TOPK):
            mx = jnp.max(g, axis=-1, keepdims=True)
            ix = jnp.min(jnp.where(g == mx, lane, LANES), axis=-1, keepdims=True)
            out = jnp.where(lane == s, ix, out)
            g = jnp.where(lane == ix, -3e38, g)
        idx_ref[0] = out


def _decode_gate(page_table, cache_kt, q, *, n_full, ppb):
    nb, d = q.shape
    n_phys, n_heads, hd, page = cache_kt.shape
    assert page == LANES and n_full <= LANES
    n_pages = page_table.shape[1]
    bps = 2
    assert n_full % bps == 0

    def page_spec(bb, r):
        return pl.BlockSpec((1, n_heads, hd, page),
                            lambda b, jj, pt: (pt[b * n_pages + (jj * bps + bb) * ppb + r], 0, 0, 0))

    pspecs = [page_spec(bb, r) for bb in range(bps) for r in range(ppb)]
    grid_spec = pltpu.PrefetchScalarGridSpec(
        num_scalar_prefetch=1,
        grid=(nb, n_full // bps),
        in_specs=pspecs + [pl.BlockSpec((1, 1, d), lambda b, jj, pt: (b, 0, 0))],
        out_specs=pl.BlockSpec((1, n_heads, LANES), lambda b, jj, pt: (b, 0, 0)),
        scratch_shapes=[pltpu.VMEM((d, LANES), F32), pltpu.VMEM((n_heads, LANES), F32)],
    )
    return pl.pallas_call(
        functools.partial(_decode_gate_kernel, n_full=n_full, ppb=ppb, bps=bps, n_heads=n_heads, hd=hd),
        grid_spec=grid_spec,
        out_shape=jax.ShapeDtypeStruct((nb, n_heads, LANES), I32),
        compiler_params=pltpu.CompilerParams(dimension_semantics=("arbitrary", "arbitrary"),
                                             vmem_limit_bytes=_vmem_limit(2 * bps * ppb * d * page * 4 + 4 * d * LANES * 4)),
        name="decode_gate",
    )(page_table.reshape(-1), *([cache_kt] * (bps * ppb)), q.reshape(nb, 1, d))


def _decode_attn_kernel(pt_ref, idx_ref, *refs, n_slots, ppb, n_full, scale):
    n_pg = n_slots * ppb
    kp = refs[0:n_pg]
    vp = refs[n_pg:2 * n_pg]
    q_ref, kn_ref, vn_ref, dec_ref, o_ref = refs[2 * n_pg:]
    bh = pl.program_id(0) * pl.num_programs(1) + pl.program_id(1)
    page = kp[0].shape[-1]
    q = q_ref[0]
    q8 = jnp.broadcast_to(q, (SUBLANES, q.shape[1])).astype(BF16)
    dec = dec_ref[0]
    s_list, v_list = [], []
    for s in range(n_slots):
        blk_id = idx_ref[bh * n_slots + s]
        bias = jnp.where(blk_id == n_full - 1, dec[0:1, :], dec[1:2, :])
        for r in range(ppb):
            kt = kp[s * ppb + r][0, 0].astype(BF16)
            sc = jnp.dot(q8, kt, preferred_element_type=F32) * scale + bias[:, r * page:(r + 1) * page]
            s_list.append(sc)
            v_list.append(vp[s * ppb + r][0, 0].astype(BF16))
    s_self = jnp.sum(q * kn_ref[0], axis=-1, keepdims=True) * scale + dec[2:3, 0:1]
    m = s_self
    for sc in s_list:
        m = jnp.maximum(m, jnp.max(sc, axis=-1, keepdims=True))
    p_self = jnp.exp(s_self - m)
    l = p_self
    acc = p_self * vn_ref[0]
    for sc, vt in zip(s_list, v_list):
        p = jnp.exp(sc - m)
        l = l + jnp.sum(p, axis=-1, keepdims=True)
        acc = acc + lax.dot_general(p.astype(BF16), vt, _NT, preferred_element_type=F32)
    o_ref[0] = (acc / l)[0:1, :]


def _decode_attn(page_table, idx_flat, cache_kt, cache_vt, q, k_new, v_new, dec, *, n_full, ppb):
    nb, d = q.shape
    n_phys, n_heads, hd, page = cache_kt.shape
    n_pages = page_table.shape[1]
    n_slots = MOBA_TOPK

    def page_spec(s, r):
        def imap(b, h, pt, idx):
            blk_id = idx[(b * n_heads + h) * n_slots + s]
            return (pt[b * n_pages + blk_id * ppb + r], h, 0, 0)
        return pl.BlockSpec((1, 1, hd, page), imap)

    head = pl.BlockSpec((1, 1, hd), lambda b, h, pt, idx: (b * n_heads + h, 0, 0))
    pspecs = [page_spec(s, r) for s in range(n_slots) for r in range(ppb)]
    grid_spec = pltpu.PrefetchScalarGridSpec(
        num_scalar_prefetch=2,
        grid=(nb, n_heads),
        in_specs=pspecs + pspecs + [head, head, head,
                                    pl.BlockSpec((1, SUBLANES, MOBA_BLOCK), lambda b, h, pt, idx: (h, 0, 0))],
        out_specs=head,
    )
    n_pg = n_slots * ppb
    per_head = lambda a: a.reshape(nb * n_heads, 1, hd)
    return pl.pallas_call(
        functools.partial(_decode_attn_kernel, n_slots=n_slots, ppb=ppb, n_full=n_full, scale=float(hd) ** -0.5),
        grid_spec=grid_spec,
        out_shape=jax.ShapeDtypeStruct((nb * n_heads, 1, hd), F32),
        compiler_params=pltpu.CompilerParams(dimension_semantics=("arbitrary", "arbitrary")),
        name="decode_attn",
    )(page_table.reshape(-1), idx_flat, *([cache_kt] * n_pg), *([cache_vt] * n_pg),
      per_head(q), per_head(k_new), per_head(v_new), dec)


def _s5_params(log_dt, a_re, a_im, b_re, b_im, c_re, c_im, oct_g):
    g, p = a_re.shape
    gd = b_re.shape[-1]
    dt = jnp.exp(log_dt.astype(F32))[:, None]
    ar, ai = a_re.astype(F32), a_im.astype(F32)
    mag = jnp.exp(dt * ar)
    abar_re, abar_im = mag * jnp.cos(dt * ai), mag * jnp.sin(dt * ai)
    den = ar * ar + ai * ai
    coef_re = ((abar_re - 1.0) * ar + abar_im * ai) / den
    coef_im = (abar_im * ar - (abar_re - 1.0) * ai) / den
    br, bim = b_re.astype(F32), b_im.astype(F32)
    bbar_re = coef_re[..., None] * br - coef_im[..., None] * bim
    bbar_im = coef_re[..., None] * bim + coef_im[..., None] * br
    n_oct = g // oct_g
    eye = jnp.eye(oct_g, dtype=F32)

    def in_blocks(bb):
        bb = bb.reshape(n_oct, oct_g, p, gd).transpose(0, 1, 3, 2)
        return jnp.einsum('ogip,gh->ogihp', bb, eye).reshape(n_oct, oct_g * gd, oct_g * p)

    def out_blocks(cc):
        cc = cc.reshape(n_oct, oct_g, gd, p).transpose(0, 1, 3, 2)
        return jnp.einsum('ogpi,gh->ogphi', cc, eye).reshape(n_oct, oct_g * p, oct_g * gd)

    return (abar_re.reshape(1, g * p), abar_im.reshape(1, g * p), in_blocks(bbar_re), in_blocks(bbar_im),
            out_blocks(c_re.astype(F32)), out_blocks(c_im.astype(F32)))


def _glu_tail(y, xn, d_ref, wg_ref, bg_ref, d):
    z = jax.nn.gelu(y + d_ref[...] * xn).astype(BF16)
    gl = jnp.dot(z, wg_ref[...], preferred_element_type=F32) + bg_ref[...]
    return gl[:, 0:d] * jax.nn.sigmoid(gl[:, d:2 * d])


def _s5_prompt_kernel(x_ref, g_ref, ar_ref, ai_ref, bre_ref, bim_ref, cre_ref, cim_ref, d_ref, wg_ref, bg_ref,
                      o_ref, hr_ref, hi_ref, xs_ref, un_ref, ub_ref, sre_ref, sim_ref, y_ref, *, n_oct, lc):
    nb, _, d = x_ref.shape
    rows = nb * lc
    ow = d // n_oct
    sw = ar_ref.shape[1] // n_oct

    @pl.when(pl.program_id(0) == 0)
    def _():
        hr_ref[...] = jnp.zeros_like(hr_ref)
        hi_ref[...] = jnp.zeros_like(hi_ref)

    xs_ref[...] = _rms(x_ref[...], g_ref[...])

    def to_time_major(t, c):
        un_ref[pl.ds(pl.multiple_of(t * nb, nb), nb), :] = xs_ref[:, pl.ds(t, 1), :].reshape(nb, d)
        return c

    lax.fori_loop(0, lc, to_time_major, 0)
    ub_ref[...] = un_ref[...].astype(BF16)

    for o in range(n_oct):
        uo = ub_ref[:, o * ow:(o + 1) * ow]
        sre_ref[...] = jnp.dot(uo, bre_ref[o], preferred_element_type=F32)
        sim_ref[...] = jnp.dot(uo, bim_ref[o], preferred_element_type=F32)
        ar = ar_ref[:, o * sw:(o + 1) * sw]
        ai = ai_ref[:, o * sw:(o + 1) * sw]

        def step(t, carry):
            hr, hi = carry
            r = pl.ds(pl.multiple_of(t * nb, nb), nb)
            nr = ar * hr - ai * hi + sre_ref[r, :]
            ni = ar * hi + ai * hr + sim_ref[r, :]
            sre_ref[r, :] = nr
            sim_ref[r, :] = ni
            return nr, ni

        hr, hi = lax.fori_loop(0, lc, step, (hr_ref[:, o * sw:(o + 1) * sw], hi_ref[:, o * sw:(o + 1) * sw]),
                               unroll=8)
        hr_ref[:, o * sw:(o + 1) * sw] = hr
        hi_ref[:, o * sw:(o + 1) * sw] = hi
        y_ref[:, o * ow:(o + 1) * ow] = (
            jnp.dot(sre_ref[...].astype(BF16), cre_ref[o], preferred_element_type=F32)
            - jnp.dot(sim_ref[...].astype(BF16), cim_ref[o], preferred_element_type=F32))

    y_ref[...] = _glu_tail(y_ref[...], un_ref[...], d_ref, wg_ref, bg_ref, d)

    def to_batch_major(t, c):
        mix = y_ref[pl.ds(pl.multiple_of(t * nb, nb), nb), :].reshape(nb, 1, d)
        o_ref[:, pl.ds(t, 1), :] = x_ref[:, pl.ds(t, 1), :] + mix
        return c

    lax.fori_loop(0, lc, to_batch_major, 0)


def _s5_prompt(x, g, prm, d_skip, w_glu, b_glu, *, n_oct):
    nb, t, d = x.shape
    assert nb == SUBLANES
    ar, ai, bre, bim, cre, cim = prm
    ns = ar.shape[1]
    lc = 64
    assert t % lc == 0
    rows = nb * lc
    ar8 = jnp.broadcast_to(ar, (nb, ns))
    ai8 = jnp.broadcast_to(ai, (nb, ns))
    xblk = pl.BlockSpec((nb, lc, d), lambda c: (0, c, 0))
    st = pl.BlockSpec((nb, ns), lambda c: (0, 0))
    need = (4 * nb * lc * d * 4 + 3 * rows * d * 4 + rows * d * 2 + 2 * rows * (ns // n_oct) * 4
            + 2 * (bre.size * 2 + cre.size * 2 + w_glu.size) + rows * 2 * d * 4 + 4 * nb * ns * 4)
    return pl.pallas_call(
        functools.partial(_s5_prompt_kernel, n_oct=n_oct, lc=lc),
        grid=(t // lc,),
        in_specs=[xblk, _const_spec((1, d)), _const_spec((nb, ns)), _const_spec((nb, ns)),
                  _const_spec(bre.shape), _const_spec(bim.shape), _const_spec(cre.shape), _const_spec(cim.shape),
                  _const_spec((1, d)), _const_spec(w_glu.shape), _const_spec((1, 2 * d))],
        out_specs=[xblk, st, st],
        out_shape=[jax.ShapeDtypeStruct((nb, t, d), F32), jax.ShapeDtypeStruct((nb, ns), F32),
                   jax.ShapeDtypeStruct((nb, ns), F32)],
        scratch_shapes=[pltpu.VMEM((nb, lc, d), F32), pltpu.VMEM((rows, d), F32), pltpu.VMEM((rows, d), BF16),
                        pltpu.VMEM((rows, ns // n_oct), F32), pltpu.VMEM((rows, ns // n_oct), F32),
                        pltpu.VMEM((rows, d), F32)],
        compiler_params=pltpu.CompilerParams(dimension_semantics=("arbitrary",),
                                             vmem_limit_bytes=_vmem_limit(need)),
        name="s5_prompt",
    )(x, g.reshape(1, d), ar8, ai8, bre.astype(BF16), bim.astype(BF16), cre.astype(BF16), cim.astype(BF16),
      d_skip.reshape(1, d), w_glu, b_glu.reshape(1, 2 * d))


def _s5_sample_kernel(x_ref, g_ref, h0r_ref, h0i_ref, ar_ref, ai_ref, bre_ref, bim_ref, cre_ref, cim_ref,
                      d_ref, wg_ref, bg_ref, o_ref, hr_ref, hi_ref, y_ref, *, n_oct):
    d = x_ref.shape[1]
    ow = d // n_oct
    sw = ar_ref.shape[1] // n_oct
    x = x_ref[...]
    xn = _rms(x, g_ref[...])
    u = _split3(xn)

    def dot3(lhs3, w):
        w1, w2, w3 = _split3(w)
        l1, l2, l3 = lhs3
        dd = lambda p, q_: jnp.dot(p, q_, preferred_element_type=F32)
        return ((dd(l1, w1) + dd(l1, w2)) + (dd(l2, w1) + dd(l1, w3))) + (dd(l2, w2) + dd(l3, w1))

    for o in range(n_oct):
        uo = tuple(p[:, o * ow:(o + 1) * ow] for p in u)
        ar = ar_ref[:, o * sw:(o + 1) * sw]
        ai = ai_ref[:, o * sw:(o + 1) * sw]
        hr = h0r_ref[:, o * sw:(o + 1) * sw]
        hi = h0i_ref[:, o * sw:(o + 1) * sw]
        nr = ar * hr - ai * hi + dot3(uo, bre_ref[o])
        ni = ar * hi + ai * hr + dot3(uo, bim_ref[o])
        hr_ref[:, o * sw:(o + 1) * sw] = nr
        hi_ref[:, o * sw:(o + 1) * sw] = ni
        y_ref[:, o * ow:(o + 1) * ow] = (
            jnp.dot(nr.astype(BF16), cre_ref[o].astype(BF16), preferred_element_type=F32)
            - jnp.dot(ni.astype(BF16), cim_ref[o].astype(BF16), preferred_element_type=F32))
    o_ref[...] = x + _glu_tail(y_ref[...], xn, d_ref, wg_ref, bg_ref, d)


def _s5_sample(x, g, h_re, h_im, prm, d_skip, w_glu, b_glu, *, n_oct):
    n, d = x.shape
    ar, ai, bre, bim, cre, cim = prm
    ns = ar.shape[1]
    full = lambda shape: pl.BlockSpec(shape, lambda i: (0,) * len(shape))
    need = 4 * (bre.size * 2 + cre.size * 2) + 2 * w_glu.size + 16 * n * ns * 4
    return pl.pallas_call(
        functools.partial(_s5_sample_kernel, n_oct=n_oct),
        grid=(1,),
        in_specs=[full((n, d)), full((1, d)), full((n, ns)), full((n, ns)), full((1, ns)), full((1, ns)),
                  _const_spec(bre.shape), _const_spec(bim.shape), _const_spec(cre.shape), _const_spec(cim.shape),
                  full((1, d)), _const_spec(w_glu.shape), full((1, 2 * d))],
        out_specs=[full((n, d)), full((n, ns)), full((n, ns))],
        out_shape=[jax.ShapeDtypeStruct((n, d), F32), jax.ShapeDtypeStruct((n, ns), F32),
                   jax.ShapeDtypeStruct((n, ns), F32)],
        scratch_shapes=[pltpu.VMEM((n, d), F32)],
        compiler_params=pltpu.CompilerParams(vmem_limit_bytes=_vmem_limit(need)),
        name="s5_sample",
    )(x, g.reshape(1, d), h_re, h_im, ar, ai, bre, bim, cre, cim, d_skip.reshape(1, d), w_glu,
      b_glu.reshape(1, 2 * d))


def kernel(x_prompt, x_sample, cache_k, cache_v, state_ssm_re, state_ssm_im, page_table, rel_bias, norm_g, final_norm_g, ffn_w_in, ffn_w_out, att_w_qkv, att_w_o, ssm_log_dt, ssm_a_re, ssm_a_im, ssm_b_re, ssm_b_im, ssm_c_re, ssm_c_im, ssm_d, ssm_w_glu, ssm_b_glu):
    bp, t, d = x_prompt.shape
    bs, ts, _ = x_sample.shape
    assert ts == 1
    n_heads = cache_k.shape[3]
    hd = cache_k.shape[4]
    page = cache_k.shape[2]
    n_phys = cache_k.shape[1]
    n_pages = page_table.shape[1]
    past_len = n_pages * page
    assert MOBA_BLOCK % page == 0 and past_len % MOBA_BLOCK == 0
    ppb = MOBA_BLOCK // page
    n_full = past_len // MOBA_BLOCK
    assert n_full >= MOBA_TOPK
    n_groups, n_state = ssm_a_re.shape[1], ssm_a_re.shape[2]
    oct_g = LANES // ssm_b_re.shape[-1]
    n_oct = n_groups // oct_g
    depth = norm_g.shape[0]
    assert depth == 2

    w_in = ffn_w_in.astype(BF16)
    w_out = ffn_w_out.astype(BF16)
    w_qkv = att_w_qkv[0].astype(BF16)
    w_o = att_w_o[0].astype(BF16)
    w_glu = ssm_w_glu[0].astype(BF16)

    xp = x_prompt.reshape(bp * t, d)
    xs = x_sample.reshape(bs, d)

    xp = _ffn(xp, norm_g[0, 0], w_in[0, 0], w_out[0, 0], name="ffn_prompt")
    xs = _ffn(xs, norm_g[0, 0], w_in[0, 0], w_out[0, 0], name="ffn_sample")

    tiles, dec = _bias_tables(rel_bias, n_heads)

    k_p, v_p, qb, kb, vb, km = _qkv_prompt(xp, norm_g[0, 1], w_qkv, float(hd) ** -0.5)
    o_p = _moba_prompt(qb.reshape(bp, t, d), kb.reshape(bp, t, d), vb.reshape(bp, t, d),
                       km.reshape(bp, t // MOBA_BLOCK, d), tiles, rel_bias, n_heads=n_heads)
    xp = _ffn(xp, norm_g[0, 2], w_in[0, 1], w_out[0, 1], pre=(o_p.reshape(bp * t, d), w_o), name="ffn_prompt_o")

    q_s, k_s, v_s = _qkv_sample(xs, norm_g[0, 1], w_qkv)
    ckt = cache_k[0].transpose(0, 2, 3, 1)
    cvt = cache_v[0].transpose(0, 2, 3, 1)
    idx = _decode_gate(page_table, ckt, q_s, n_full=n_full, ppb=ppb)
    idx_flat = idx[:, :, :MOBA_TOPK].reshape(-1)
    o_s = _decode_attn(page_table, idx_flat, ckt, cvt, q_s, k_s, v_s, dec, n_full=n_full, ppb=ppb)
    xs = _ffn(xs, norm_g[0, 2], w_in[0, 1], w_out[0, 1], pre=(o_s.reshape(bs, d).astype(BF16), w_o),
              name="ffn_sample_o")

    xp = _ffn(xp, norm_g[1, 0], w_in[1, 0], w_out[1, 0], name="ffn_prompt_l1")
    xs = _ffn(xs, norm_g[1, 0], w_in[1, 0], w_out[1, 0], name="ffn_sample_l1")
    prm = _s5_params(ssm_log_dt[0], ssm_a_re[0], ssm_a_im[0], ssm_b_re[0], ssm_b_im[0], ssm_c_re[0], ssm_c_im[0],
                     oct_g)
    xp3, hr_p, hi_p = _s5_prompt(xp.reshape(bp, t, d), norm_g[1, 1], prm, ssm_d[0], w_glu, ssm_b_glu[0], n_oct=n_oct)
    xs, hr_s, hi_s = _s5_sample(xs, norm_g[1, 1], state_ssm_re[0].reshape(bs, n_groups * n_state),
                                state_ssm_im[0].reshape(bs, n_groups * n_state), prm, ssm_d[0], w_glu,
                                ssm_b_glu[0], n_oct=n_oct)
    xp = _ffn(xp3.reshape(bp * t, d), norm_g[1, 2], w_in[1, 1], w_out[1, 1], final_g=final_norm_g,
              name="ffn_prompt_final")
    xs = _ffn(xs, norm_g[1, 2], w_in[1, 1], w_out[1, 1], final_g=final_norm_g, name="ffn_sample_final")

    st = lambda a, b: a.reshape(1, b, n_groups, n_state)
    return (xp.reshape(bp, t, d), xs.reshape(bs, 1, d),
            k_p.reshape(1, bp, t, n_heads, hd), v_p.reshape(1, bp, t, n_heads, hd),
            k_s.reshape(1, bs, 1, n_heads, hd), v_s.reshape(1, bs, 1, n_heads, hd),
            st(hr_p, bp), st(hi_p, bp), st(hr_s, bs), st(hi_s, bs))
```

```python
import functools
import math

import numpy as np
import jax
import jax.numpy as jnp
from jax import lax
from jax.experimental import pallas as pl
from jax.experimental.pallas import tpu as pltpu

F32 = jnp.float32
BF16 = jnp.bfloat16
I32 = jnp.int32

RMS_EPS = 1e-6
NEG = -1e30
MOBA_BLOCK = 256
MOBA_TOPK = 3
REL_MAX_DIST = 128
LANES = 128
SUBLANES = 8
VMEM_CAP = 60 * 1024 * 1024

_NT = (((1,), (1,)), ((), ()))


def _vmem_limit(nbytes):
    return int(min(VMEM_CAP, nbytes + (8 << 20)))


def _const_spec(shape):
    nd = len(shape)
    return pl.BlockSpec(shape, lambda *_: (0,) * nd, pipeline_mode=pl.Buffered(1))


def _rms(x, g):
    return x * lax.rsqrt(jnp.mean(x * x, axis=-1, keepdims=True) + RMS_EPS) * g


def _split3(x):
    a = x.astype(BF16)
    r = x - a.astype(F32)
    b = r.astype(BF16)
    c = (r - b.astype(F32)).astype(BF16)
    return a, b, c


def _ffn_kernel(*refs, d_ff, chunk, has_pre, has_final):
    it = iter(refs)
    x_ref = next(it)
    if has_pre:
        a_ref, wpre_ref = next(it), next(it)
    g_ref, win_ref, wout_ref = next(it), next(it), next(it)
    if has_final:
        gf_ref = next(it)
    o_ref, act_ref = next(it), next(it)

    x = x_ref[...]
    if has_pre:
        x = x + jnp.dot(a_ref[...], wpre_ref[...], preferred_element_type=F32)
    xn = _rms(x, g_ref[...]).astype(BF16)
    for c in range(d_ff // chunk):
        lo = c * chunk
        gate = jnp.dot(xn, win_ref[:, lo:lo + chunk], preferred_element_type=F32)
        up = jnp.dot(xn, win_ref[:, d_ff + lo:d_ff + lo + chunk], preferred_element_type=F32)
        act_ref[:, lo:lo + chunk] = (gate * jax.nn.sigmoid(gate) * up).astype(BF16)
    y = x + 0.5 * jnp.dot(act_ref[...], wout_ref[...], preferred_element_type=F32)
    if has_final:
        y = _rms(y, gf_ref[...])
    o_ref[...] = y


def _ffn(x, g, w_in, w_out, *, pre=None, final_g=None, name):
    n, d = x.shape
    d_ff = w_out.shape[0]
    tm = min(n, 512)
    chunk = 256
    assert n % tm == 0 and d_ff % chunk == 0
    row = lambda i: (i, 0)
    args, specs = [x], [pl.BlockSpec((tm, d), row)]
    if pre is not None:
        a, w_pre = pre
        args += [a, w_pre]
        specs += [pl.BlockSpec((tm, a.shape[1]), row), _const_spec(w_pre.shape)]
    args += [g.reshape(1, d), w_in, w_out]
    specs += [_const_spec((1, d)), _const_spec(w_in.shape), _const_spec(w_out.shape)]
    if final_g is not None:
        args.append(final_g.reshape(1, d))
        specs.append(_const_spec((1, d)))
    wbytes = 2 * (w_in.size + w_out.size + (pre[1].size if pre is not None else 0))
    need = wbytes + tm * d * 4 * 4 + tm * d_ff * 2 + tm * chunk * 4 * 6 + tm * d * 4 * 3
    return pl.pallas_call(
        functools.partial(_ffn_kernel, d_ff=d_ff, chunk=chunk, has_pre=pre is not None,
                          has_final=final_g is not None),
        grid=(n // tm,),
        in_specs=specs,
        out_specs=pl.BlockSpec((tm, d), row),
        out_shape=jax.ShapeDtypeStruct((n, d), F32),
        scratch_shapes=[pltpu.VMEM((tm, d_ff), BF16)],
        compiler_params=pltpu.CompilerParams(dimension_semantics=("parallel",),
                                             vmem_limit_bytes=_vmem_limit(need)),
        name=name,
    )(*args)


def _qkv_prompt_kernel(x_ref, g_ref, w_ref, kt_ref, vt_ref, qb_ref, kb_ref, vb_ref, km_ref, *, d, scale):
    xn = _rms(x_ref[...], g_ref[...]).astype(BF16)
    q = jnp.dot(xn, w_ref[:, 0:d], preferred_element_type=F32)
    qb_ref[...] = (q * scale).astype(BF16)
    k = jnp.dot(xn, w_ref[:, d:2 * d], preferred_element_type=F32)
    kt_ref[0] = k.T
    kb_ref[...] = k.astype(BF16)
    for r in range(k.shape[0] // MOBA_BLOCK):
        km_ref[r] = jnp.sum(k[r * MOBA_BLOCK:(r + 1) * MOBA_BLOCK], axis=0, keepdims=True) * (1.0 / MOBA_BLOCK)
    v = jnp.dot(xn, w_ref[:, 2 * d:3 * d], preferred_element_type=F32)
    vt_ref[0] = v.T
    vb_ref[...] = v.astype(BF16)


def _qkv_prompt(x, g, w_qkv, scale, t):
    n, d = x.shape
    tm = 512
    assert n % t == 0 and t % tm == 0 and tm % MOBA_BLOCK == 0
    tpb = t // tm
    row = lambda i: (i, 0)
    rs = pl.BlockSpec((tm, d), row)
    ts = pl.BlockSpec((1, d, tm), lambda i: (i // tpb, 0, i % tpb))
    need = 2 * w_qkv.size + tm * d * (4 * 2 * 3 + 2 * 2 * 3) + tm * d * 4 * 6
    return pl.pallas_call(
        functools.partial(_qkv_prompt_kernel, d=d, scale=scale),
        grid=(n // tm,),
        in_specs=[rs, _const_spec((1, d)), _const_spec(w_qkv.shape)],
        out_specs=[ts, ts, rs, rs, rs,
                   pl.BlockSpec((tm // MOBA_BLOCK, 1, d), lambda i: (i, 0, 0))],
        out_shape=[jax.ShapeDtypeStruct((n // t, d, t), F32), jax.ShapeDtypeStruct((n // t, d, t), F32),
                   jax.ShapeDtypeStruct((n, d), BF16), jax.ShapeDtypeStruct((n, d), BF16),
                   jax.ShapeDtypeStruct((n, d), BF16),
                   jax.ShapeDtypeStruct((n // MOBA_BLOCK, 1, d), F32)],
        compiler_params=pltpu.CompilerParams(dimension_semantics=("parallel",),
                                             vmem_limit_bytes=_vmem_limit(need)),
        name="qkv_prompt",
    )(x, g.reshape(1, d), w_qkv)


def _qkv_sample_kernel(x_ref, g_ref, w_ref, q_ref, k_ref, v_ref, *, d):
    xn = _rms(x_ref[...], g_ref[...]).astype(BF16)
    q_ref[...] = jnp.dot(xn, w_ref[:, 0:d], preferred_element_type=F32)
    k_ref[...] = jnp.dot(xn, w_ref[:, d:2 * d], preferred_element_type=F32)
    v_ref[...] = jnp.dot(xn, w_ref[:, 2 * d:3 * d], preferred_element_type=F32)


def _qkv_sample(x, g, w_qkv):
    n, d = x.shape
    full = pl.BlockSpec((n, d), lambda i: (0, 0))
    sd = jax.ShapeDtypeStruct((n, d), F32)
    return pl.pallas_call(
        functools.partial(_qkv_sample_kernel, d=d),
        grid=(1,),
        in_specs=[full, _const_spec((1, d)), _const_spec(w_qkv.shape)],
        out_specs=[full, full, full],
        out_shape=[sd, sd, sd],
        compiler_params=pltpu.CompilerParams(vmem_limit_bytes=_vmem_limit(2 * w_qkv.size + 16 * n * d)),
        name="qkv_sample",
    )(x, g.reshape(1, d), w_qkv)


def _rel_bucket_np(n, n_buckets):
    n = np.maximum(n, 0)
    max_exact = n_buckets // 2
    nf = np.maximum(n, max_exact).astype(np.float32)
    large = max_exact + (np.log(nf / np.float32(max_exact)) / np.float32(math.log(REL_MAX_DIST / max_exact))
                         * np.float32(n_buckets - max_exact)).astype(np.int32)
    large = np.minimum(large, n_buckets - 1)
    return np.where(n < max_exact, n, large).astype(np.int32)


def _bias_kernel(rel_ref, bkt_ref, dbk_ref, tile_ref, dec_ref, *, n_buckets):
    h = pl.program_id(0)
    for t in range(2):
        b = bkt_ref[t]
        acc = jnp.full(b.shape, NEG, F32)
        for u in range(n_buckets):
            acc = jnp.where(b == u, rel_ref[u, h], acc)
        tile_ref[0, t] = acc
    b = dbk_ref[...]
    acc = jnp.zeros(b.shape, F32)
    for u in range(n_buckets):
        acc = jnp.where(b == u, rel_ref[u, h], acc)
    dec_ref[0] = acc


def _bias_tables(rel_bias, n_heads):
    n_buckets = rel_bias.shape[0]
    blk = MOBA_BLOCK
    qi = np.arange(blk)[:, None]
    ki = np.arange(blk)[None, :]
    own = np.where(ki <= qi, _rel_bucket_np(qi - ki, n_buckets), -1)
    adj = _rel_bucket_np(blk + qi - ki, n_buckets)
    bkt = np.stack([own, adj]).astype(np.int32)
    dbk = np.zeros((SUBLANES, blk), np.int32)
    dbk[0] = _rel_bucket_np(blk - np.arange(blk), n_buckets)
    dbk[1] = _rel_bucket_np(np.full((blk,), 2 * blk), n_buckets)
    dbk[2] = 0
    return pl.pallas_call(
        functools.partial(_bias_kernel, n_buckets=n_buckets),
        grid=(n_heads,),
        in_specs=[pl.BlockSpec(memory_space=pltpu.SMEM),
                  pl.BlockSpec((2, blk, blk), lambda h: (0, 0, 0)),
                  pl.BlockSpec((SUBLANES, blk), lambda h: (0, 0))],
        out_specs=[pl.BlockSpec((1, 2, blk, blk), lambda h: (h, 0, 0, 0)),
                   pl.BlockSpec((1, SUBLANES, blk), lambda h: (h, 0, 0))],
        out_shape=[jax.ShapeDtypeStruct((n_heads, 2, blk, blk), F32),
                   jax.ShapeDtypeStruct((n_heads, SUBLANES, blk), F32)],
        name="rel_bias_tables",
    )(rel_bias, jnp.asarray(bkt), jnp.asarray(dbk))


def _moba_kernel(rel_ref, q_ref, k_ref, v_ref, km_ref, tile_ref, o_ref, q2_ref, s_ref, m_ref, acc_ref, *,
                 n_blk, hd, far_bucket, unroll):
    blk = MOBA_BLOCK
    hp = pl.program_id(1)
    i = pl.program_id(2)
    q = q_ref[0]
    lane = lax.broadcasted_iota(I32, (blk, LANES), 1)
    q2 = jnp.concatenate([jnp.where(lane < hd, q, jnp.zeros_like(q)),
                          jnp.where(lane >= hd, q, jnp.zeros_like(q))], axis=0)
    km = km_ref[0]
    km_hi = km.astype(BF16)
    km_lo = (km - km_hi.astype(F32)).astype(BF16)
    gt = (lax.dot_general(km_hi, q2, _NT, preferred_element_type=F32)
          + lax.dot_general(km_lo, q2, _NT, preferred_element_type=F32))
    bidx = lax.broadcasted_iota(I32, (n_blk, 2 * blk), 0)
    qcol = lax.broadcasted_iota(I32, (n_blk, 2 * blk), 1)
    valid = bidx < i
    gm = jnp.where(valid, gt, NEG)
    rank = jnp.zeros((n_blk, 2 * blk), I32)
    for jp in range(n_blk):
        row = gm[jp:jp + 1, :]
        rank = rank + jnp.where(row > gm, 1, jnp.where(row == gm, jnp.where(bidx > jp, 1, 0), 0))
    far = jnp.where(qcol < blk, rel_ref[far_bucket, 2 * hp], rel_ref[far_bucket, 2 * hp + 1])
    chosen = jnp.where(rank < MOBA_TOPK, jnp.where(bidx <= i - 2, far, 0.0), NEG)
    aval = jnp.where(valid, chosen, NEG)
    a_hi = aval.astype(BF16).astype(F32)
    a_lo = (aval - a_hi).astype(BF16).astype(F32)
    at = jnp.concatenate([a_hi, a_lo, jnp.zeros((LANES - 2 * n_blk, 2 * blk), F32)], axis=0)
    q2_ref[:, 0:LANES] = q2
    q2_ref[:, LANES:2 * LANES] = at.T.astype(BF16)

    ones_col = jnp.where(lane == 0, 1.0, 0.0).astype(BF16)
    ja = jnp.maximum(i - 1, 0)
    n_far = jnp.maximum(i - 1, 0)
    n_it = (n_far + unroll - 1) // unroll

    def k_aug(j):
        kj = k_ref[0, pl.ds(pl.multiple_of(j * blk, blk), blk), :]
        ej = jnp.where((lane == j) | (lane == n_blk + j), 1.0, 0.0).astype(BF16)
        return jnp.concatenate([kj, ej], axis=1)

    def v_aug(j):
        vj = v_ref[0, pl.ds(pl.multiple_of(j * blk, blk), blk), :]
        return jnp.concatenate([vj, ones_col], axis=1)

    def far_block(j):
        return jnp.where(j >= n_far, n_blk - 1, j)

    def lane_max(s):
        return jnp.maximum(s[:, 0:LANES], s[:, LANES:2 * LANES])

    k_own = k_ref[0, pl.ds(pl.multiple_of(i * blk, blk), blk), :]
    s_own = (lax.dot_general(q2, k_own, _NT, preferred_element_type=F32)
             + jnp.concatenate([tile_ref[0, 0], tile_ref[1, 0]], axis=0))
    s_adj = (lax.dot_general(q2_ref[...], k_aug(ja), _NT, preferred_element_type=F32)
             + jnp.concatenate([tile_ref[0, 1], tile_ref[1, 1]], axis=0))
    s_ref[0] = s_own
    s_ref[1] = s_adj
    m_ref[...] = jnp.maximum(lane_max(s_own), lane_max(s_adj))

    def pass_a(it, c):
        mm = m_ref[...]
        for u in range(unroll):
            j = it * unroll + u
            s = lax.dot_general(q2_ref[...], k_aug(far_block(j)), _NT, preferred_element_type=F32)
            s_ref[2 + j] = s
            mm = jnp.maximum(mm, lane_max(s))
        m_ref[...] = mm
        return c

    lax.fori_loop(0, n_it, pass_a, 0)
    m_ref[...] = jnp.broadcast_to(jnp.max(m_ref[...], axis=-1, keepdims=True), m_ref.shape)

    def prob(slot):
        mb = m_ref[...]
        return jnp.exp(s_ref[slot] - jnp.concatenate([mb, mb], axis=1)).astype(BF16)

    acc_ref[...] = (jnp.dot(prob(0), v_aug(i), preferred_element_type=F32)
                    + jnp.dot(prob(1), v_aug(ja), preferred_element_type=F32))

    def pass_b(it, c):
        part = None
        for u in range(unroll):
            j = it * unroll + u
            d_ = jnp.dot(prob(2 + j), v_aug(far_block(j)), preferred_element_type=F32)
            part = d_ if part is None else part + d_
        acc_ref[...] += part
        return c

    lax.fori_loop(0, n_it, pass_b, 0)
    acc = acc_ref[...]
    o0 = acc[0:blk, 0:LANES] / acc[0:blk, LANES:LANES + 1]
    o1 = acc[blk:2 * blk, 0:LANES] / acc[blk:2 * blk, LANES:LANES + 1]
    o_ref[0] = jnp.where(lane < hd, o0, o1).astype(BF16)


def _moba_prompt(qb, kb, vb, km, tiles, rel_bias, *, n_heads):
    b, t, d = qb.shape
    hd = d // n_heads
    assert 2 * hd == LANES and t % MOBA_BLOCK == 0
    n_blk = t // MOBA_BLOCK
    assert 2 * n_blk <= LANES
    blk = MOBA_BLOCK
    far_bucket = int(_rel_bucket_np(np.array([2 * blk]), rel_bias.shape[0])[0])
    assert int(_rel_bucket_np(np.array([blk + 1]), rel_bias.shape[0])[0]) == far_bucket
    seq = pl.BlockSpec((1, t, LANES), lambda bi, hp, i: (bi, 0, hp))
    qblk = pl.BlockSpec((1, blk, LANES), lambda bi, hp, i: (bi, i, hp))
    unroll = 4
    n_slots = 2 + -(-(n_blk - 1) // unroll) * unroll
    need = (2 * 2 * t * LANES * 2 + n_slots * 2 * blk * blk * 4 + 2 * 4 * blk * blk * 4
            + 2 * blk * (2 * LANES * 2 + LANES * 4 + 2 * LANES * 4) + (8 << 20))
    return pl.pallas_call(
        functools.partial(_moba_kernel, n_blk=n_blk, hd=hd, far_bucket=far_bucket, unroll=unroll),
        grid=(b, n_heads // 2, n_blk),
        in_specs=[pl.BlockSpec(memory_space=pltpu.SMEM), qblk, seq, seq,
                  pl.BlockSpec((1, n_blk, LANES), lambda bi, hp, i: (bi, 0, hp)),
                  pl.BlockSpec((2, 2, blk, blk), lambda bi, hp, i: (hp, 0, 0, 0))],
        out_specs=qblk,
        out_shape=jax.ShapeDtypeStruct((b, t, d), BF16),
        scratch_shapes=[pltpu.VMEM((2 * blk, 2 * LANES), BF16), pltpu.VMEM((n_slots, 2 * blk, blk), F32),
                        pltpu.VMEM((2 * blk, LANES), F32), pltpu.VMEM((2 * blk, 2 * LANES), F32)],
        compiler_params=pltpu.CompilerParams(dimension_semantics=("parallel", "parallel", "arbitrary"),
                                             vmem_limit_bytes=_vmem_limit(need)),
        name="moba_prompt",
    )(rel_bias, qb, kb, vb, km, tiles)


def _decode_gate_kernel(pt_ref, *refs, n_full, ppb, bps, n_heads, hd):
    n_pg = bps * ppb
    pages = refs[0:n_pg]
    q_ref, idx_ref, qb_ref, g_ref = refs[n_pg:]
    jj = pl.program_id(1)
    lane = lax.broadcasted_iota(I32, (n_heads, LANES), 1)

    @pl.when(jj == 0)
    def _():
        q128 = jnp.broadcast_to(q_ref[0], (LANES, n_heads * hd))
        for c in range(n_heads * hd // LANES):
            qb_ref[c * LANES:(c + 1) * LANES, :] = q128[:, c * LANES:(c + 1) * LANES].T
        g_ref[...] = jnp.zeros_like(g_ref)

    head = lax.broadcasted_iota(I32, (n_heads, LANES), 0)
    g = g_ref[...]
    for h in range(n_heads):
        accs = [None] * bps
        for c in range(hd // SUBLANES):
            rows = slice(c * SUBLANES, (c + 1) * SUBLANES)
            qv = qb_ref[h * hd + c * SUBLANES:h * hd + (c + 1) * SUBLANES, :]
            for bb in range(bps):
                kk = pages[bb * ppb][0, h, rows, :]
                for r in range(1, ppb):
                    kk = kk + pages[bb * ppb + r][0, h, rows, :]
                accs[bb] = kk * qv if accs[bb] is None else accs[bb] + kk * qv
        for bb in range(bps):
            tot = jnp.sum(jnp.sum(accs[bb], axis=0, keepdims=True), axis=-1, keepdims=True) * (1.0 / MOBA_BLOCK)
            g = jnp.where(lane == jj * bps + bb, jnp.where(head == h, tot, g), g)
    g_ref[...] = g

    @pl.when(jj == pl.num_programs(1) - 1)
    def _():
        g = jnp.where(lane < n_full, g_ref[...], -3e38)
        out = jnp.zeros((n_heads, LANES), I32)
        for s in range(MOBA_TOPK):
            mx = jnp.max(g, axis=-1, keepdims=True)
            ix = jnp.min(jnp.where(g == mx, lane, LANES), axis=-1, keepdims=True)
            out = jnp.where(lane == s, ix, out)
            g = jnp.where(lane == ix, -3e38, g)
        idx_ref[0] = out


def _decode_gate(page_table, cache_kt, q, *, n_full, ppb):
    nb, d = q.shape
    n_phys, n_heads, hd, page = cache_kt.shape
    assert page == LANES and n_full <= LANES
    n_pages = page_table.shape[1]
    bps = 4
    assert n_full % bps == 0

    def page_spec(bb, r):
        return pl.BlockSpec((1, n_heads, hd, page),
                            lambda b, jj, pt: (pt[b * n_pages + (jj * bps + bb) * ppb + r], 0, 0, 0))

    pspecs = [page_spec(bb, r) for bb in range(bps) for r in range(ppb)]
    grid_spec = pltpu.PrefetchScalarGridSpec(
        num_scalar_prefetch=1,
        grid=(nb, n_full // bps),
        in_specs=pspecs + [pl.BlockSpec((1, 1, d), lambda b, jj, pt: (b, 0, 0))],
        out_specs=pl.BlockSpec((1, n_heads, LANES), lambda b, jj, pt: (b, 0, 0)),
        scratch_shapes=[pltpu.VMEM((d, LANES), F32), pltpu.VMEM((n_heads, LANES), F32)],
    )
    return pl.pallas_call(
        functools.partial(_decode_gate_kernel, n_full=n_full, ppb=ppb, bps=bps, n_heads=n_heads, hd=hd),
        grid_spec=grid_spec,
        out_shape=jax.ShapeDtypeStruct((nb, n_heads, LANES), I32),
        compiler_params=pltpu.CompilerParams(dimension_semantics=("arbitrary", "arbitrary"),
                                             vmem_limit_bytes=_vmem_limit(2 * bps * ppb * d * page * 4 + 4 * d * LANES * 4)),
        name="decode_gate",
    )(page_table.reshape(-1), *([cache_kt] * (bps * ppb)), q.reshape(nb, 1, d))


def _decode_attn_kernel(pt_ref, idx_ref, *refs, n_slots, ppb, n_full, scale):
    n_pg = n_slots * ppb
    kp = refs[0:n_pg]
    vp = refs[n_pg:2 * n_pg]
    q_ref, kn_ref, vn_ref, dec_ref, o_ref = refs[2 * n_pg:]
    bh = pl.program_id(0) * pl.num_programs(1) + pl.program_id(1)
    page = kp[0].shape[-1]
    q = q_ref[0]
    q8 = jnp.broadcast_to(q, (SUBLANES, q.shape[1])).astype(BF16)
    dec = dec_ref[0]
    s_list, v_list = [], []
    for s in range(n_slots):
        blk_id = idx_ref[bh * n_slots + s]
        bias = jnp.where(blk_id == n_full - 1, dec[0:1, :], dec[1:2, :])
        for r in range(ppb):
            kt = kp[s * ppb + r][0, 0].astype(BF16)
            sc = jnp.dot(q8, kt, preferred_element_type=F32) * scale + bias[:, r * page:(r + 1) * page]
            s_list.append(sc)
            v_list.append(vp[s * ppb + r][0, 0].astype(BF16))
    s_self = jnp.sum(q * kn_ref[0], axis=-1, keepdims=True) * scale + dec[2:3, 0:1]
    m = s_self
    for sc in s_list:
        m = jnp.maximum(m, jnp.max(sc, axis=-1, keepdims=True))
    p_self = jnp.exp(s_self - m)
    l = p_self
    acc = p_self * vn_ref[0]
    for sc, vt in zip(s_list, v_list):
        p = jnp.exp(sc - m)
        l = l + jnp.sum(p, axis=-1, keepdims=True)
        acc = acc + lax.dot_general(p.astype(BF16), vt, _NT, preferred_element_type=F32)
    o_ref[0] = (acc / l)[0:1, :]


def _decode_attn(page_table, idx_flat, cache_kt, cache_vt, q, k_new, v_new, dec, *, n_full, ppb):
    nb, d = q.shape
    n_phys, n_heads, hd, page = cache_kt.shape
    n_pages = page_table.shape[1]
    n_slots = MOBA_TOPK

    def page_spec(s, r):
        def imap(b, h, pt, idx):
            blk_id = idx[(b * n_heads + h) * n_slots + s]
            return (pt[b * n_pages + blk_id * ppb + r], h, 0, 0)
        return pl.BlockSpec((1, 1, hd, page), imap)

    head = pl.BlockSpec((1, 1, hd), lambda b, h, pt, idx: (b * n_heads + h, 0, 0))
    pspecs = [page_spec(s, r) for s in range(n_slots) for r in range(ppb)]
    grid_spec = pltpu.PrefetchScalarGridSpec(
        num_scalar_prefetch=2,
        grid=(nb, n_heads),
        in_specs=pspecs + pspecs + [head, head, head,
                                    pl.BlockSpec((1, SUBLANES, MOBA_BLOCK), lambda b, h, pt, idx: (h, 0, 0))],
        out_specs=head,
    )
    n_pg = n_slots * ppb
    per_head = lambda a: a.reshape(nb * n_heads, 1, hd)
    return pl.pallas_call(
        functools.partial(_decode_attn_kernel, n_slots=n_slots, ppb=ppb, n_full=n_full, scale=float(hd) ** -0.5),
        grid_spec=grid_spec,
        out_shape=jax.ShapeDtypeStruct((nb * n_heads, 1, hd), F32),
        compiler_params=pltpu.CompilerParams(dimension_semantics=("arbitrary", "arbitrary")),
        name="decode_attn",
    )(page_table.reshape(-1), idx_flat, *([cache_kt] * n_pg), *([cache_vt] * n_pg),
      per_head(q), per_head(k_new), per_head(v_new), dec)


def _s5_params(log_dt, a_re, a_im, b_re, b_im, c_re, c_im, oct_g):
    g, p = a_re.shape
    gd = b_re.shape[-1]
    dt = jnp.exp(log_dt.astype(F32))[:, None]
    ar, ai = a_re.astype(F32), a_im.astype(F32)
    mag = jnp.exp(dt * ar)
    abar_re, abar_im = mag * jnp.cos(dt * ai), mag * jnp.sin(dt * ai)
    den = ar * ar + ai * ai
    coef_re = ((abar_re - 1.0) * ar + abar_im * ai) / den
    coef_im = (abar_im * ar - (abar_re - 1.0) * ai) / den
    br, bim = b_re.astype(F32), b_im.astype(F32)
    bbar_re = coef_re[..., None] * br - coef_im[..., None] * bim
    bbar_im = coef_re[..., None] * bim + coef_im[..., None] * br
    n_oct = g // oct_g
    eye = jnp.eye(oct_g, dtype=F32)

    def in_blocks(bb):
        bb = bb.reshape(n_oct, oct_g, p, gd).transpose(0, 1, 3, 2)
        return jnp.einsum('ogip,gh->ogihp', bb, eye).reshape(n_oct, oct_g * gd, oct_g * p)

    def out_blocks(cc):
        cc = cc.reshape(n_oct, oct_g, gd, p).transpose(0, 1, 3, 2)
        return jnp.einsum('ogpi,gh->ogphi', cc, eye).reshape(n_oct, oct_g * p, oct_g * gd)

    return (abar_re.reshape(1, g * p), abar_im.reshape(1, g * p), in_blocks(bbar_re), in_blocks(bbar_im),
            out_blocks(c_re.astype(F32)), out_blocks(c_im.astype(F32)))


def _glu_tail(y, xn, d_ref, wg_ref, bg_ref, d):
    z = jax.nn.gelu(y + d_ref[...] * xn).astype(BF16)
    gl = jnp.dot(z, wg_ref[...], preferred_element_type=F32) + bg_ref[...]
    return gl[:, 0:d] * jax.nn.sigmoid(gl[:, d:2 * d])


def _s5_prompt_kernel(x_ref, g_ref, ar_ref, ai_ref, bre_ref, bim_ref, cre_ref, cim_ref, d_ref, wg_ref, bg_ref,
                      o_ref, hr_ref, hi_ref, xs_ref, un_ref, ub_ref, sre_ref, sim_ref, y_ref, *, n_oct, lc):
    nb, _, d = x_ref.shape
    rows = nb * lc
    ow = d // n_oct
    sw = ar_ref.shape[1] // n_oct

    @pl.when(pl.program_id(0) == 0)
    def _():
        hr_ref[...] = jnp.zeros_like(hr_ref)
        hi_ref[...] = jnp.zeros_like(hi_ref)

    xs_ref[...] = _rms(x_ref[...], g_ref[...])

    def to_time_major(t, c):
        un_ref[pl.ds(pl.multiple_of(t * nb, nb), nb), :] = xs_ref[:, pl.ds(t, 1), :].reshape(nb, d)
        return c

    lax.fori_loop(0, lc, to_time_major, 0)
    ub_ref[...] = un_ref[...].astype(BF16)

    for o in range(n_oct):
        uo = ub_ref[:, o * ow:(o + 1) * ow]
        sre_ref[...] = jnp.dot(uo, bre_ref[o], preferred_element_type=F32)
        sim_ref[...] = jnp.dot(uo, bim_ref[o], preferred_element_type=F32)
        ar = ar_ref[:, o * sw:(o + 1) * sw]
        ai = ai_ref[:, o * sw:(o + 1) * sw]

        def step(t, carry):
            hr, hi = carry
            r = pl.ds(pl.multiple_of(t * nb, nb), nb)
            nr = ar * hr - ai * hi + sre_ref[r, :]
            ni = ar * hi + ai * hr + sim_ref[r, :]
            sre_ref[r, :] = nr
            sim_ref[r, :] = ni
            return nr, ni

        hr, hi = lax.fori_loop(0, lc, step, (hr_ref[:, o * sw:(o + 1) * sw], hi_ref[:, o * sw:(o + 1) * sw]),
                               unroll=True)
        hr_ref[:, o * sw:(o + 1) * sw] = hr
        hi_ref[:, o * sw:(o + 1) * sw] = hi
        y_ref[:, o * ow:(o + 1) * ow] = (
            jnp.dot(sre_ref[...].astype(BF16), cre_ref[o], preferred_element_type=F32)
            - jnp.dot(sim_ref[...].astype(BF16), cim_ref[o], preferred_element_type=F32))

    y_ref[...] = _glu_tail(y_ref[...], un_ref[...], d_ref, wg_ref, bg_ref, d)

    def to_batch_major(t, c):
        mix = y_ref[pl.ds(pl.multiple_of(t * nb, nb), nb), :].reshape(nb, 1, d)
        o_ref[:, pl.ds(t, 1), :] = x_ref[:, pl.ds(t, 1), :] + mix
        return c

    lax.fori_loop(0, lc, to_batch_major, 0)


def _s5_prompt(x, g, prm, d_skip, w_glu, b_glu, *, n_oct):
    nb, t, d = x.shape
    assert nb == SUBLANES
    ar, ai, bre, bim, cre, cim = prm
    ns = ar.shape[1]
    lc = 64
    assert t % lc == 0
    rows = nb * lc
    ar8 = jnp.broadcast_to(ar, (nb, ns))
    ai8 = jnp.broadcast_to(ai, (nb, ns))
    xblk = pl.BlockSpec((nb, lc, d), lambda c: (0, c, 0))
    st = pl.BlockSpec((nb, ns), lambda c: (0, 0))
    need = (4 * nb * lc * d * 4 + 3 * rows * d * 4 + rows * d * 2 + 2 * rows * (ns // n_oct) * 4
            + 2 * (bre.size * 2 + cre.size * 2 + w_glu.size) + rows * 2 * d * 4 + 4 * nb * ns * 4)
    return pl.pallas_call(
        functools.partial(_s5_prompt_kernel, n_oct=n_oct, lc=lc),
        grid=(t // lc,),
        in_specs=[xblk, _const_spec((1, d)), _const_spec((nb, ns)), _const_spec((nb, ns)),
                  _const_spec(bre.shape), _const_spec(bim.shape), _const_spec(cre.shape), _const_spec(cim.shape),
                  _const_spec((1, d)), _const_spec(w_glu.shape), _const_spec((1, 2 * d))],
        out_specs=[xblk, st, st],
        out_shape=[jax.ShapeDtypeStruct((nb, t, d), F32), jax.ShapeDtypeStruct((nb, ns), F32),
                   jax.ShapeDtypeStruct((nb, ns), F32)],
        scratch_shapes=[pltpu.VMEM((nb, lc, d), F32), pltpu.VMEM((rows, d), F32), pltpu.VMEM((rows, d), BF16),
                        pltpu.VMEM((rows, ns // n_oct), F32), pltpu.VMEM((rows, ns // n_oct), F32),
                        pltpu.VMEM((rows, d), F32)],
        compiler_params=pltpu.CompilerParams(dimension_semantics=("arbitrary",),
                                             vmem_limit_bytes=_vmem_limit(need)),
        name="s5_prompt",
    )(x, g.reshape(1, d), ar8, ai8, bre.astype(BF16), bim.astype(BF16), cre.astype(BF16), cim.astype(BF16),
      d_skip.reshape(1, d), w_glu, b_glu.reshape(1, 2 * d))


def _s5_sample_kernel(x_ref, g_ref, h0r_ref, h0i_ref, ar_ref, ai_ref, bre_ref, bim_ref, cre_ref, cim_ref,
                      d_ref, wg_ref, bg_ref, o_ref, hr_ref, hi_ref, y_ref, *, n_oct):
    d = x_ref.shape[1]
    ow = d // n_oct
    sw = ar_ref.shape[1] // n_oct
    x = x_ref[...]
    xn = _rms(x, g_ref[...])
    u = _split3(xn)

    def dot3(lhs3, w):
        w1, w2, w3 = _split3(w)
        l1, l2, l3 = lhs3
        dd = lambda p, q_: jnp.dot(p, q_, preferred_element_type=F32)
        return ((dd(l1, w1) + dd(l1, w2)) + (dd(l2, w1) + dd(l1, w3))) + (dd(l2, w2) + dd(l3, w1))

    for o in range(n_oct):
        uo = tuple(p[:, o * ow:(o + 1) * ow] for p in u)
        ar = ar_ref[:, o * sw:(o + 1) * sw]
        ai = ai_ref[:, o * sw:(o + 1) * sw]
        hr = h0r_ref[:, o * sw:(o + 1) * sw]
        hi = h0i_ref[:, o * sw:(o + 1) * sw]
        nr = ar * hr - ai * hi + dot3(uo, bre_ref[o])
        ni = ar * hi + ai * hr + dot3(uo, bim_ref[o])
        hr_ref[:, o * sw:(o + 1) * sw] = nr
        hi_ref[:, o * sw:(o + 1) * sw] = ni
        y_ref[:, o * ow:(o + 1) * ow] = (
            jnp.dot(nr.astype(BF16), cre_ref[o].astype(BF16), preferred_element_type=F32)
            - jnp.dot(ni.astype(BF16), cim_ref[o].astype(BF16), preferred_element_type=F32))
    o_ref[...] = x + _glu_tail(y_ref[...], xn, d_ref, wg_ref, bg_ref, d)


def _s5_sample(x, g, h_re, h_im, prm, d_skip, w_glu, b_glu, *, n_oct):
    n, d = x.shape
    ar, ai, bre, bim, cre, cim = prm
    ns = ar.shape[1]
    full = lambda shape: pl.BlockSpec(shape, lambda i: (0,) * len(shape))
    need = 4 * (bre.size * 2 + cre.size * 2) + 2 * w_glu.size + 16 * n * ns * 4
    return pl.pallas_call(
        functools.partial(_s5_sample_kernel, n_oct=n_oct),
        grid=(1,),
        in_specs=[full((n, d)), full((1, d)), full((n, ns)), full((n, ns)), full((1, ns)), full((1, ns)),
                  _const_spec(bre.shape), _const_spec(bim.shape), _const_spec(cre.shape), _const_spec(cim.shape),
                  full((1, d)), _const_spec(w_glu.shape), full((1, 2 * d))],
        out_specs=[full((n, d)), full((n, ns)), full((n, ns))],
        out_shape=[jax.ShapeDtypeStruct((n, d), F32), jax.ShapeDtypeStruct((n, ns), F32),
                   jax.ShapeDtypeStruct((n, ns), F32)],
        scratch_shapes=[pltpu.VMEM((n, d), F32)],
        compiler_params=pltpu.CompilerParams(vmem_limit_bytes=_vmem_limit(need)),
        name="s5_sample",
    )(x, g.reshape(1, d), h_re, h_im, ar, ai, bre, bim, cre, cim, d_skip.reshape(1, d), w_glu,
      b_glu.reshape(1, 2 * d))


def kernel(x_prompt, x_sample, cache_k, cache_v, state_ssm_re, state_ssm_im, page_table, rel_bias, norm_g, final_norm_g, ffn_w_in, ffn_w_out, att_w_qkv, att_w_o, ssm_log_dt, ssm_a_re, ssm_a_im, ssm_b_re, ssm_b_im, ssm_c_re, ssm_c_im, ssm_d, ssm_w_glu, ssm_b_glu):
    bp, t, d = x_prompt.shape
    bs, ts, _ = x_sample.shape
    assert ts == 1
    n_heads = cache_k.shape[3]
    hd = cache_k.shape[4]
    page = cache_k.shape[2]
    n_phys = cache_k.shape[1]
    n_pages = page_table.shape[1]
    past_len = n_pages * page
    assert MOBA_BLOCK % page == 0 and past_len % MOBA_BLOCK == 0
    ppb = MOBA_BLOCK // page
    n_full = past_len // MOBA_BLOCK
    assert n_full >= MOBA_TOPK
    n_groups, n_state = ssm_a_re.shape[1], ssm_a_re.shape[2]
    oct_g = LANES // ssm_b_re.shape[-1]
    n_oct = n_groups // oct_g
    depth = norm_g.shape[0]
    assert depth == 2

    w_in = ffn_w_in.astype(BF16)
    w_out = ffn_w_out.astype(BF16)
    w_qkv = att_w_qkv[0].astype(BF16)
    w_o = att_w_o[0].astype(BF16)
    w_glu = ssm_w_glu[0].astype(BF16)

    xp = x_prompt.reshape(bp * t, d)
    xs = x_sample.reshape(bs, d)

    xp = _ffn(xp, norm_g[0, 0], w_in[0, 0], w_out[0, 0], name="ffn_prompt")
    xs = _ffn(xs, norm_g[0, 0], w_in[0, 0], w_out[0, 0], name="ffn_sample")

    tiles, dec = _bias_tables(rel_bias, n_heads)

    kt_p, vt_p, qb, kb, vb, km = _qkv_prompt(xp, norm_g[0, 1], w_qkv, float(hd) ** -0.5, t)
    o_p = _moba_prompt(qb.reshape(bp, t, d), kb.reshape(bp, t, d), vb.reshape(bp, t, d),
                       km.reshape(bp, t // MOBA_BLOCK, d), tiles, rel_bias, n_heads=n_heads)
    xp = _ffn(xp, norm_g[0, 2], w_in[0, 1], w_out[0, 1], pre=(o_p.reshape(bp * t, d), w_o), name="ffn_prompt_o")

    q_s, k_s, v_s = _qkv_sample(xs, norm_g[0, 1], w_qkv)
    ckt = cache_k[0].transpose(0, 2, 3, 1)
    cvt = cache_v[0].transpose(0, 2, 3, 1)
    idx = _decode_gate(page_table, ckt, q_s, n_full=n_full, ppb=ppb)
    idx_flat = idx[:, :, :MOBA_TOPK].reshape(-1)
    o_s = _decode_attn(page_table, idx_flat, ckt, cvt, q_s, k_s, v_s, dec, n_full=n_full, ppb=ppb)
    xs = _ffn(xs, norm_g[0, 2], w_in[0, 1], w_out[0, 1], pre=(o_s.reshape(bs, d).astype(BF16), w_o),
              name="ffn_sample_o")

    xp = _ffn(xp, norm_g[1, 0], w_in[1, 0], w_out[1, 0], name="ffn_prompt_l1")
    xs = _ffn(xs, norm_g[1, 0], w_in[1, 0], w_out[1, 0], name="ffn_sample_l1")
    prm = _s5_params(ssm_log_dt[0], ssm_a_re[0], ssm_a_im[0], ssm_b_re[0], ssm_b_im[0], ssm_c_re[0], ssm_c_im[0],
                     oct_g)
    xp3, hr_p, hi_p = _s5_prompt(xp.reshape(bp, t, d), norm_g[1, 1], prm, ssm_d[0], w_glu, ssm_b_glu[0], n_oct=n_oct)
    xs, hr_s, hi_s = _s5_sample(xs, norm_g[1, 1], state_ssm_re[0].reshape(bs, n_groups * n_state),
                                state_ssm_im[0].reshape(bs, n_groups * n_state), prm, ssm_d[0], w_glu,
                                ssm_b_glu[0], n_oct=n_oct)
    xp = _ffn(xp3.reshape(bp * t, d), norm_g[1, 2], w_in[1, 1], w_out[1, 1], final_g=final_norm_g,
              name="ffn_prompt_final")
    xs = _ffn(xs, norm_g[1, 2], w_in[1, 1], w_out[1, 1], final_g=final_norm_g, name="ffn_sample_final")

    st = lambda a, b: a.reshape(1, b, n_groups, n_state)
    tok_major = lambda a: a.reshape(1, bp, n_heads, hd, t).transpose(0, 1, 4, 2, 3)
    return (xp.reshape(bp, t, d), xs.reshape(bs, 1, d), tok_major(kt_p), tok_major(vt_p),
            k_s.reshape(1, bs, 1, n_heads, hd), v_s.reshape(1, bs, 1, n_heads, hd),
            st(hr_p, bp), st(hi_p, bp), st(hr_s, bs), st(hi_s, bs))
```

```python
import functools
import math

import numpy as np
import jax
import jax.numpy as jnp
from jax import lax
from jax.experimental import pallas as pl
from jax.experimental.pallas import tpu as pltpu

F32 = jnp.float32
BF16 = jnp.bfloat16
I32 = jnp.int32

RMS_EPS = 1e-6
NEG = -1e30
MOBA_BLOCK = 256
MOBA_TOPK = 3
REL_MAX_DIST = 128
LANES = 128
SUBLANES = 8
VMEM_CAP = 60 * 1024 * 1024

_NT = (((1,), (1,)), ((), ()))


def _vmem_limit(nbytes):
    return int(min(VMEM_CAP, nbytes + (8 << 20)))


def _const_spec(shape):
    nd = len(shape)
    return pl.BlockSpec(shape, lambda *_: (0,) * nd, pipeline_mode=pl.Buffered(1))


def _rms(x, g):
    return x * lax.rsqrt(jnp.mean(x * x, axis=-1, keepdims=True) + RMS_EPS) * g


def _split3(x):
    a = x.astype(BF16)
    r = x - a.astype(F32)
    b = r.astype(BF16)
    c = (r - b.astype(F32)).astype(BF16)
    return a, b, c


def _ffn_kernel(*refs, d_ff, chunk, has_pre, has_final):
    it = iter(refs)
    x_ref = next(it)
    if has_pre:
        a_ref, wpre_ref = next(it), next(it)
    g_ref, win_ref, wout_ref = next(it), next(it), next(it)
    if has_final:
        gf_ref = next(it)
    o_ref, act_ref = next(it), next(it)

    x = x_ref[...]
    if has_pre:
        x = x + jnp.dot(a_ref[...], wpre_ref[...], preferred_element_type=F32)
    xn = _rms(x, g_ref[...]).astype(BF16)
    for c in range(d_ff // chunk):
        lo = c * chunk
        gate = jnp.dot(xn, win_ref[:, lo:lo + chunk], preferred_element_type=F32)
        up = jnp.dot(xn, win_ref[:, d_ff + lo:d_ff + lo + chunk], preferred_element_type=F32)
        act_ref[:, lo:lo + chunk] = (gate * jax.nn.sigmoid(gate) * up).astype(BF16)
    y = x + 0.5 * jnp.dot(act_ref[...], wout_ref[...], preferred_element_type=F32)
    if has_final:
        y = _rms(y, gf_ref[...])
    o_ref[...] = y


def _ffn(x, g, w_in, w_out, *, pre=None, final_g=None, name):
    n, d = x.shape
    d_ff = w_out.shape[0]
    tm = min(n, 512)
    chunk = 256
    assert n % tm == 0 and d_ff % chunk == 0
    row = lambda i: (i, 0)
    args, specs = [x], [pl.BlockSpec((tm, d), row)]
    if pre is not None:
        a, w_pre = pre
        args += [a, w_pre]
        specs += [pl.BlockSpec((tm, a.shape[1]), row), _const_spec(w_pre.shape)]
    args += [g.reshape(1, d), w_in, w_out]
    specs += [_const_spec((1, d)), _const_spec(w_in.shape), _const_spec(w_out.shape)]
    if final_g is not None:
        args.append(final_g.reshape(1, d))
        specs.append(_const_spec((1, d)))
    wbytes = 2 * (w_in.size + w_out.size + (pre[1].size if pre is not None else 0))
    need = wbytes + tm * d * 4 * 4 + tm * d_ff * 2 + tm * chunk * 4 * 6 + tm * d * 4 * 3
    return pl.pallas_call(
        functools.partial(_ffn_kernel, d_ff=d_ff, chunk=chunk, has_pre=pre is not None,
                          has_final=final_g is not None),
        grid=(n // tm,),
        in_specs=specs,
        out_specs=pl.BlockSpec((tm, d), row),
        out_shape=jax.ShapeDtypeStruct((n, d), F32),
        scratch_shapes=[pltpu.VMEM((tm, d_ff), BF16)],
        compiler_params=pltpu.CompilerParams(dimension_semantics=("parallel",),
                                             vmem_limit_bytes=_vmem_limit(need)),
        name=name,
    )(*args)


def _qkv_prompt_kernel(x_ref, g_ref, w_ref, kt_ref, vt_ref, qb_ref, kb_ref, vb_ref, km_ref, *, d, scale):
    xn = _rms(x_ref[...], g_ref[...]).astype(BF16)
    q = jnp.dot(xn, w_ref[:, 0:d], preferred_element_type=F32)
    qb_ref[...] = (q * scale).astype(BF16)
    k = jnp.dot(xn, w_ref[:, d:2 * d], preferred_element_type=F32)
    kt_ref[0] = k.T
    kb_ref[...] = k.astype(BF16)
    for r in range(k.shape[0] // MOBA_BLOCK):
        km_ref[r] = jnp.sum(k[r * MOBA_BLOCK:(r + 1) * MOBA_BLOCK], axis=0, keepdims=True) * (1.0 / MOBA_BLOCK)
    v = jnp.dot(xn, w_ref[:, 2 * d:3 * d], preferred_element_type=F32)
    vt_ref[0] = v.T
    vb_ref[...] = v.astype(BF16)


def _qkv_prompt(x, g, w_qkv, scale, t):
    n, d = x.shape
    tm = 512
    assert n % t == 0 and t % tm == 0 and tm % MOBA_BLOCK == 0
    tpb = t // tm
    row = lambda i: (i, 0)
    rs = pl.BlockSpec((tm, d), row)
    ts = pl.BlockSpec((1, d, tm), lambda i: (i // tpb, 0, i % tpb))
    need = 2 * w_qkv.size + tm * d * (4 * 2 * 3 + 2 * 2 * 3) + tm * d * 4 * 6
    return pl.pallas_call(
        functools.partial(_qkv_prompt_kernel, d=d, scale=scale),
        grid=(n // tm,),
        in_specs=[rs, _const_spec((1, d)), _const_spec(w_qkv.shape)],
        out_specs=[ts, ts, rs, rs, rs,
                   pl.BlockSpec((tm // MOBA_BLOCK, 1, d), lambda i: (i, 0, 0))],
        out_shape=[jax.ShapeDtypeStruct((n // t, d, t), F32), jax.ShapeDtypeStruct((n // t, d, t), F32),
                   jax.ShapeDtypeStruct((n, d), BF16), jax.ShapeDtypeStruct((n, d), BF16),
                   jax.ShapeDtypeStruct((n, d), BF16),
                   jax.ShapeDtypeStruct((n // MOBA_BLOCK, 1, d), F32)],
        compiler_params=pltpu.CompilerParams(dimension_semantics=("parallel",),
                                             vmem_limit_bytes=_vmem_limit(need)),
        name="qkv_prompt",
    )(x, g.reshape(1, d), w_qkv)


def _qkv_sample_kernel(x_ref, g_ref, w_ref, q_ref, k_ref, v_ref, *, d):
    xn = _rms(x_ref[...], g_ref[...]).astype(BF16)
    q_ref[...] = jnp.dot(xn, w_ref[:, 0:d], preferred_element_type=F32)
    k_ref[...] = jnp.dot(xn, w_ref[:, d:2 * d], preferred_element_type=F32)
    v_ref[...] = jnp.dot(xn, w_ref[:, 2 * d:3 * d], preferred_element_type=F32)


def _qkv_sample(x, g, w_qkv):
    n, d = x.shape
    full = pl.BlockSpec((n, d), lambda i: (0, 0))
    sd = jax.ShapeDtypeStruct((n, d), F32)
    return pl.pallas_call(
        functools.partial(_qkv_sample_kernel, d=d),
        grid=(1,),
        in_specs=[full, _const_spec((1, d)), _const_spec(w_qkv.shape)],
        out_specs=[full, full, full],
        out_shape=[sd, sd, sd],
        compiler_params=pltpu.CompilerParams(vmem_limit_bytes=_vmem_limit(2 * w_qkv.size + 16 * n * d)),
        name="qkv_sample",
    )(x, g.reshape(1, d), w_qkv)


def _rel_bucket_np(n, n_buckets):
    n = np.maximum(n, 0)
    max_exact = n_buckets // 2
    nf = np.maximum(n, max_exact).astype(np.float32)
    large = max_exact + (np.log(nf / np.float32(max_exact)) / np.float32(math.log(REL_MAX_DIST / max_exact))
                         * np.float32(n_buckets - max_exact)).astype(np.int32)
    large = np.minimum(large, n_buckets - 1)
    return np.where(n < max_exact, n, large).astype(np.int32)


def _bias_kernel(rel_ref, bkt_ref, dbk_ref, tile_ref, dec_ref, *, n_buckets):
    h = pl.program_id(0)
    for t in range(2):
        b = bkt_ref[t]
        acc = jnp.full(b.shape, NEG, F32)
        for u in range(n_buckets):
            acc = jnp.where(b == u, rel_ref[u, h], acc)
        tile_ref[0, t] = acc
    b = dbk_ref[...]
    acc = jnp.zeros(b.shape, F32)
    for u in range(n_buckets):
        acc = jnp.where(b == u, rel_ref[u, h], acc)
    dec_ref[0] = acc


def _bias_tables(rel_bias, n_heads):
    n_buckets = rel_bias.shape[0]
    blk = MOBA_BLOCK
    qi = np.arange(blk)[:, None]
    ki = np.arange(blk)[None, :]
    own = np.where(ki <= qi, _rel_bucket_np(qi - ki, n_buckets), -1)
    adj = _rel_bucket_np(blk + qi - ki, n_buckets)
    bkt = np.stack([own, adj]).astype(np.int32)
    dbk = np.zeros((SUBLANES, blk), np.int32)
    dbk[0] = _rel_bucket_np(blk - np.arange(blk), n_buckets)
    dbk[1] = _rel_bucket_np(np.full((blk,), 2 * blk), n_buckets)
    dbk[2] = 0
    return pl.pallas_call(
        functools.partial(_bias_kernel, n_buckets=n_buckets),
        grid=(n_heads,),
        in_specs=[pl.BlockSpec(memory_space=pltpu.SMEM),
                  pl.BlockSpec((2, blk, blk), lambda h: (0, 0, 0)),
                  pl.BlockSpec((SUBLANES, blk), lambda h: (0, 0))],
        out_specs=[pl.BlockSpec((1, 2, blk, blk), lambda h: (h, 0, 0, 0)),
                   pl.BlockSpec((1, SUBLANES, blk), lambda h: (h, 0, 0))],
        out_shape=[jax.ShapeDtypeStruct((n_heads, 2, blk, blk), F32),
                   jax.ShapeDtypeStruct((n_heads, SUBLANES, blk), F32)],
        name="rel_bias_tables",
    )(rel_bias, jnp.asarray(bkt), jnp.asarray(dbk))


def _moba_kernel(rel_ref, q_ref, k_ref, v_ref, km_ref, tile_ref, o_ref, q2_ref, s_ref, m_ref, acc_ref, *,
                 n_blk, hd, far_bucket):
    blk = MOBA_BLOCK
    half = 2 * blk
    rows = 2 * half
    dead = 2 * n_blk
    hp = pl.program_id(1)
    p = pl.program_id(2)
    lane = lax.broadcasted_iota(I32, (blk, LANES), 1)

    def two_heads(q):
        return [jnp.where(lane < hd, q, jnp.zeros_like(q)), jnp.where(lane >= hd, q, jnp.zeros_like(q))]

    q2 = jnp.concatenate(two_heads(q_ref[0, 0:blk, :]) + two_heads(q_ref[0, blk:2 * blk, :]), axis=0)
    km = km_ref[0]
    km_hi = km.astype(BF16)
    km_lo = (km - km_hi.astype(F32)).astype(BF16)
    gt = (lax.dot_general(km_hi, q2, _NT, preferred_element_type=F32)
          + lax.dot_general(km_lo, q2, _NT, preferred_element_type=F32))
    bidx = lax.broadcasted_iota(I32, (n_blk, rows), 0)
    qcol = lax.broadcasted_iota(I32, (n_blk, rows), 1)
    i_col = 2 * p + jnp.where(qcol >= half, 1, 0)
    valid = bidx < i_col
    gm = jnp.where(valid, gt, NEG)
    rank = jnp.zeros((n_blk, rows), I32)
    for jp in range(n_blk):
        row = gm[jp:jp + 1, :]
        rank = rank + jnp.where(row > gm, 1, jnp.where(row == gm, jnp.where(bidx > jp, 1, 0), 0))
    far = jnp.where((qcol & blk) == 0, rel_ref[far_bucket, 2 * hp], rel_ref[far_bucket, 2 * hp + 1])
    chosen = jnp.where(rank < MOBA_TOPK, jnp.where(bidx <= i_col - 2, far, 0.0), NEG)
    aval = jnp.where(valid, chosen, jnp.where(bidx == i_col, 0.0, NEG))
    a_hi = aval.astype(BF16).astype(F32)
    a_lo = (aval - a_hi).astype(BF16).astype(F32)
    dead_rows = jnp.where(lax.broadcasted_iota(I32, (SUBLANES, rows), 0) == 0, NEG, 0.0)
    at = jnp.concatenate([a_hi, a_lo, dead_rows, jnp.zeros((LANES - dead - SUBLANES, rows), F32)], axis=0)
    q2_ref[:, 0:LANES] = q2
    q2_ref[:, LANES:2 * LANES] = at.T.astype(BF16)

    ones_col = jnp.where(lane == 0, 1.0, 0.0).astype(BF16)
    n_far = jnp.maximum(2 * p - 1, 0)

    def scores(j, masked):
        kj = k_ref[0, pl.ds(pl.multiple_of(j * blk, blk), blk), :]
        c_hi = jnp.where(masked, dead, j)
        c_lo = jnp.where(masked, dead, n_blk + j)
        ej = jnp.where((lane == c_hi) | (lane == c_lo), 1.0, 0.0).astype(BF16)
        return lax.dot_general(q2_ref[...], jnp.concatenate([kj, ej], axis=1), _NT, preferred_element_type=F32)

    def v_aug(j):
        vj = v_ref[0, pl.ds(pl.multiple_of(j * blk, blk), blk), :]
        return jnp.concatenate([vj, ones_col], axis=1)

    def lane_max(s):
        return jnp.maximum(s[:, 0:LANES], s[:, LANES:2 * LANES])

    def far_id(j):
        pad = j >= n_far
        return jnp.where(pad, n_blk - 1, j), pad

    t_own = jnp.concatenate([tile_ref[0, 0], tile_ref[1, 0]], axis=0)
    t_adj = jnp.concatenate([tile_ref[0, 1], tile_ref[1, 1]], axis=0)
    s0 = scores(jnp.maximum(2 * p - 1, 0), p == 0)
    s1 = scores(2 * p, False)
    s2 = scores(2 * p + 1, False)
    s0a = s0[0:half] + t_adj
    s1a = s1[0:half] + t_own
    s1b = s1[half:rows] + t_adj
    s2b = s2[half:rows] + t_own
    s_ref[0, 0:half] = s0a
    s_ref[0, half:rows] = s0[half:rows]
    s_ref[1, 0:half] = s1a
    s_ref[1, half:rows] = s1b
    s_ref[2, 0:half] = s2[0:half]
    s_ref[2, half:rows] = s2b
    m_ref[0:half] = jnp.maximum(jnp.maximum(lane_max(s0a), lane_max(s1a)), lane_max(s2[0:half]))
    m_ref[half:rows] = jnp.maximum(jnp.maximum(lane_max(s0[half:rows]), lane_max(s1b)), lane_max(s2b))

    n4 = (n_far + 1) // 4
    tail2 = n_far - 4 * n4 == 1

    def a_group(base, width):
        mm = m_ref[...]
        for u in range(width):
            j = base + u
            s = scores(*far_id(j))
            s_ref[3 + j] = s
            mm = jnp.maximum(mm, lane_max(s))
        m_ref[...] = mm

    def a_trip(it, c):
        a_group(it * 4, 4)
        return c

    lax.fori_loop(0, n4, a_trip, 0)

    @pl.when(tail2)
    def _():
        a_group(4 * n4, 2)

    m_ref[...] = jnp.broadcast_to(jnp.max(m_ref[...], axis=-1, keepdims=True), m_ref.shape)

    def prob(slot):
        mb = m_ref[...]
        return jnp.exp(s_ref[slot] - jnp.concatenate([mb, mb], axis=1)).astype(BF16)

    acc_ref[...] = ((jnp.dot(prob(0), v_aug(jnp.maximum(2 * p - 1, 0)), preferred_element_type=F32)
                     + jnp.dot(prob(1), v_aug(2 * p), preferred_element_type=F32))
                    + jnp.dot(prob(2), v_aug(2 * p + 1), preferred_element_type=F32))

    def b_group(base, width):
        part = None
        for u in range(width):
            j = base + u
            d_ = jnp.dot(prob(3 + j), v_aug(far_id(j)[0]), preferred_element_type=F32)
            part = d_ if part is None else part + d_
        acc_ref[...] += part

    def b_trip(it, c):
        b_group(it * 4, 4)
        return c

    lax.fori_loop(0, n4, b_trip, 0)

    @pl.when(tail2)
    def _():
        b_group(4 * n4, 2)

    acc = acc_ref[...]
    for qb in range(2):
        r0 = qb * half
        o0 = acc[r0:r0 + blk, 0:LANES] / acc[r0:r0 + blk, LANES:LANES + 1]
        o1 = acc[r0 + blk:r0 + half, 0:LANES] / acc[r0 + blk:r0 + half, LANES:LANES + 1]
        o_ref[0, qb * blk:(qb + 1) * blk, :] = jnp.where(lane < hd, o0, o1).astype(BF16)


def _moba_prompt(qb, kb, vb, km, tiles, rel_bias, *, n_heads):
    b, t, d = qb.shape
    hd = d // n_heads
    assert 2 * hd == LANES and t % MOBA_BLOCK == 0
    n_blk = t // MOBA_BLOCK
    assert 2 * n_blk + SUBLANES <= LANES and n_blk % 2 == 0
    blk = MOBA_BLOCK
    rows = 4 * blk
    far_bucket = int(_rel_bucket_np(np.array([2 * blk]), rel_bias.shape[0])[0])
    assert int(_rel_bucket_np(np.array([blk + 1]), rel_bias.shape[0])[0]) == far_bucket
    seq = pl.BlockSpec((1, t, LANES), lambda bi, hp, p: (bi, 0, hp))
    qblk = pl.BlockSpec((1, 2 * blk, LANES), lambda bi, hp, p: (bi, p, hp))
    n_slots = 3 + n_blk
    need = (2 * 2 * t * LANES * 2 + n_slots * rows * blk * 4 + 2 * 4 * blk * blk * 4
            + rows * (2 * LANES * 2 + LANES * 4 + 2 * LANES * 4) + (8 << 20))
    return pl.pallas_call(
        functools.partial(_moba_kernel, n_blk=n_blk, hd=hd, far_bucket=far_bucket),
        grid=(b, n_heads // 2, n_blk // 2),
        in_specs=[pl.BlockSpec(memory_space=pltpu.SMEM), qblk, seq, seq,
                  pl.BlockSpec((1, n_blk, LANES), lambda bi, hp, p: (bi, 0, hp)),
                  pl.BlockSpec((2, 2, blk, blk), lambda bi, hp, p: (hp, 0, 0, 0))],
        out_specs=qblk,
        out_shape=jax.ShapeDtypeStruct((b, t, d), BF16),
        scratch_shapes=[pltpu.VMEM((rows, 2 * LANES), BF16), pltpu.VMEM((n_slots, rows, blk), F32),
                        pltpu.VMEM((rows, LANES), F32), pltpu.VMEM((rows, 2 * LANES), F32)],
        compiler_params=pltpu.CompilerParams(dimension_semantics=("parallel", "parallel", "arbitrary"),
                                             vmem_limit_bytes=_vmem_limit(need)),
        name="moba_prompt",
    )(rel_bias, qb, kb, vb, km, tiles)


def _decode_gate_kernel(pt_ref, *refs, n_full, ppb, bps, n_heads, hd):
    n_pg = bps * ppb
    pages = refs[0:n_pg]
    q_ref, idx_ref, qb_ref, g_ref = refs[n_pg:]
    jj = pl.program_id(1)
    lane = lax.broadcasted_iota(I32, (n_heads, LANES), 1)

    @pl.when(jj == 0)
    def _():
        q128 = jnp.broadcast_to(q_ref[0], (LANES, n_heads * hd))
        for c in range(n_heads * hd // LANES):
            qb_ref[c * LANES:(c + 1) * LANES, :] = q128[:, c * LANES:(c + 1) * LANES].T
        g_ref[...] = jnp.zeros_like(g_ref)

    head = lax.broadcasted_iota(I32, (n_heads, LANES), 0)
    g = g_ref[...]
    for h in range(n_heads):
        accs = [None] * bps
        for c in range(hd // SUBLANES):
            rows = slice(c * SUBLANES, (c + 1) * SUBLANES)
            qv = qb_ref[h * hd + c * SUBLANES:h * hd + (c + 1) * SUBLANES, :]
            for bb in range(bps):
                kk = pages[bb * ppb][0, h, rows, :]
                for r in range(1, ppb):
                    kk = kk + pages[bb * ppb + r][0, h, rows, :]
                accs[bb] = kk * qv if accs[bb] is None else accs[bb] + kk * qv
        for bb in range(bps):
            tot = jnp.sum(jnp.sum(accs[bb], axis=0, keepdims=True), axis=-1, keepdims=True) * (1.0 / MOBA_BLOCK)
            g = jnp.where(lane == jj * bps + bb, jnp.where(head == h, tot, g), g)
    g_ref[...] = g

    @pl.when(jj == pl.num_programs(1) - 1)
    def _():
        g = jnp.where(lane < n_full, g_ref[...], -3e38)
        out = jnp.zeros((n_heads, LANES), I32)
        for s in range(MOBA_TOPK):
            mx = jnp.max(g, axis=-1, keepdims=True)
            ix = jnp.min(jnp.where(g == mx, lane, LANES), axis=-1, keepdims=True)
            out = jnp.where(lane == s, ix, out)
            g = jnp.where(lane == ix, -3e38, g)
        idx_ref[0] = out


def _decode_gate(page_table, cache_kt, q, *, n_full, ppb):
    nb, d = q.shape
    n_phys, n_heads, hd, page = cache_kt.shape
    assert page == LANES and n_full <= LANES
    n_pages = page_table.shape[1]
    bps = 4
    assert n_full % bps == 0

    def page_spec(bb, r):
        return pl.BlockSpec((1, n_heads, hd, page),
                            lambda b, jj, pt: (pt[b * n_pages + (jj * bps + bb) * ppb + r], 0, 0, 0))

    pspecs = [page_spec(bb, r) for bb in range(bps) for r in range(ppb)]
    grid_spec = pltpu.PrefetchScalarGridSpec(
        num_scalar_prefetch=1,
        grid=(nb, n_full // bps),
        in_specs=pspecs + [pl.BlockSpec((1, 1, d), lambda b, jj, pt: (b, 0, 0))],
        out_specs=pl.BlockSpec((1, n_heads, LANES), lambda b, jj, pt: (b, 0, 0)),
        scratch_shapes=[pltpu.VMEM((d, LANES), F32), pltpu.VMEM((n_heads, LANES), F32)],
    )
    return pl.pallas_call(
        functools.partial(_decode_gate_kernel, n_full=n_full, ppb=ppb, bps=bps, n_heads=n_heads, hd=hd),
        grid_spec=grid_spec,
        out_shape=jax.ShapeDtypeStruct((nb, n_heads, LANES), I32),
        compiler_params=pltpu.CompilerParams(dimension_semantics=("arbitrary", "arbitrary"),
                                             vmem_limit_bytes=_vmem_limit(2 * bps * ppb * d * page * 4 + 4 * d * LANES * 4)),
        name="decode_gate",
    )(page_table.reshape(-1), *([cache_kt] * (bps * ppb)), q.reshape(nb, 1, d))


def _decode_attn_kernel(pt_ref, idx_ref, *refs, n_slots, ppb, n_full, hps, hd, scale):
    n_pg = hps * n_slots * ppb
    kp = refs[0:n_pg]
    vp = refs[n_pg:2 * n_pg]
    q_ref, kn_ref, vn_ref, dec_ref, o_ref = refs[2 * n_pg:]
    bg = pl.program_id(0) * pl.num_programs(1) + pl.program_id(1)
    page = kp[0].shape[-1]
    outs = []
    for hh in range(hps):
        cols = slice(hh * hd, (hh + 1) * hd)
        q = q_ref[0][:, cols]
        q8 = jnp.broadcast_to(q, (SUBLANES, hd)).astype(BF16)
        dec = dec_ref[hh]
        s_list, v_list = [], []
        for s in range(n_slots):
            blk_id = idx_ref[(bg * hps + hh) * n_slots + s]
            bias = jnp.where(blk_id == n_full - 1, dec[0:1, :], dec[1:2, :])
            for r in range(ppb):
                pg = (hh * n_slots + s) * ppb + r
                kt = kp[pg][0, 0].astype(BF16)
                sc = jnp.dot(q8, kt, preferred_element_type=F32) * scale + bias[:, r * page:(r + 1) * page]
                s_list.append(sc)
                v_list.append(vp[pg][0, 0].astype(BF16))
        s_self = jnp.sum(q * kn_ref[0][:, cols], axis=-1, keepdims=True) * scale + dec[2:3, 0:1]
        m = s_self
        for sc in s_list:
            m = jnp.maximum(m, jnp.max(sc, axis=-1, keepdims=True))
        p_self = jnp.exp(s_self - m)
        l = p_self
        acc = p_self * vn_ref[0][:, cols]
        for sc, vt in zip(s_list, v_list):
            p = jnp.exp(sc - m)
            l = l + jnp.sum(p, axis=-1, keepdims=True)
            acc = acc + lax.dot_general(p.astype(BF16), vt, _NT, preferred_element_type=F32)
        outs.append((acc / l)[0:1, :])
    o_ref[0] = jnp.concatenate(outs, axis=1)


def _decode_attn(page_table, idx_flat, cache_kt, cache_vt, q, k_new, v_new, dec, *, n_full, ppb):
    nb, d = q.shape
    n_phys, n_heads, hd, page = cache_kt.shape
    n_pages = page_table.shape[1]
    n_slots = MOBA_TOPK
    hps = 2
    assert n_heads % hps == 0
    n_grp = n_heads // hps

    def page_spec(hh, s, r):
        def imap(b, g, pt, idx):
            h = g * hps + hh
            blk_id = idx[(b * n_heads + h) * n_slots + s]
            return (pt[b * n_pages + blk_id * ppb + r], h, 0, 0)
        return pl.BlockSpec((1, 1, hd, page), imap)

    grp = pl.BlockSpec((1, 1, hps * hd), lambda b, g, pt, idx: (b * n_grp + g, 0, 0))
    pspecs = [page_spec(hh, s, r) for hh in range(hps) for s in range(n_slots) for r in range(ppb)]
    grid_spec = pltpu.PrefetchScalarGridSpec(
        num_scalar_prefetch=2,
        grid=(nb, n_grp),
        in_specs=pspecs + pspecs + [grp, grp, grp,
                                    pl.BlockSpec((hps, SUBLANES, MOBA_BLOCK), lambda b, g, pt, idx: (g, 0, 0))],
        out_specs=grp,
    )
    n_pg = hps * n_slots * ppb
    per_grp = lambda a: a.reshape(nb * n_grp, 1, hps * hd)
    return pl.pallas_call(
        functools.partial(_decode_attn_kernel, n_slots=n_slots, ppb=ppb, n_full=n_full, hps=hps, hd=hd,
                          scale=float(hd) ** -0.5),
        grid_spec=grid_spec,
        out_shape=jax.ShapeDtypeStruct((nb * n_grp, 1, hps * hd), F32),
        compiler_params=pltpu.CompilerParams(dimension_semantics=("arbitrary", "arbitrary")),
        name="decode_attn",
    )(page_table.reshape(-1), idx_flat, *([cache_kt] * n_pg), *([cache_vt] * n_pg),
      per_grp(q), per_grp(k_new), per_grp(v_new), dec)


def _s5_params(log_dt, a_re, a_im, b_re, b_im, c_re, c_im, oct_g):
    g, p = a_re.shape
    gd = b_re.shape[-1]
    dt = jnp.exp(log_dt.astype(F32))[:, None]
    ar, ai = a_re.astype(F32), a_im.astype(F32)
    mag = jnp.exp(dt * ar)
    abar_re, abar_im = mag * jnp.cos(dt * ai), mag * jnp.sin(dt * ai)
    den = ar * ar + ai * ai
    coef_re = ((abar_re - 1.0) * ar + abar_im * ai) / den
    coef_im = (abar_im * ar - (abar_re - 1.0) * ai) / den
    br, bim = b_re.astype(F32), b_im.astype(F32)
    bbar_re = coef_re[..., None] * br - coef_im[..., None] * bim
    bbar_im = coef_re[..., None] * bim + coef_im[..., None] * br
    n_oct = g // oct_g
    eye = jnp.eye(oct_g, dtype=F32)

    def in_blocks(bb):
        bb = bb.reshape(n_oct, oct_g, p, gd).transpose(0, 1, 3, 2)
        return jnp.einsum('ogip,gh->ogihp', bb, eye).reshape(n_oct, oct_g * gd, oct_g * p)

    def out_blocks(cc):
        cc = cc.reshape(n_oct, oct_g, gd, p).transpose(0, 1, 3, 2)
        return jnp.einsum('ogpi,gh->ogphi', cc, eye).reshape(n_oct, oct_g * p, oct_g * gd)

    return (abar_re.reshape(1, g * p), abar_im.reshape(1, g * p), in_blocks(bbar_re), in_blocks(bbar_im),
            out_blocks(c_re.astype(F32)), out_blocks(c_im.astype(F32)))


def _glu_tail(y, xn, d_ref, wg_ref, bg_ref, d):
    z = jax.nn.gelu(y + d_ref[...] * xn).astype(BF16)
    gl = jnp.dot(z, wg_ref[...], preferred_element_type=F32) + bg_ref[...]
    return gl[:, 0:d] * jax.nn.sigmoid(gl[:, d:2 * d])


def _s5_prompt_kernel(x_ref, g_ref, ar_ref, ai_ref, bre_ref, bim_ref, cre_ref, cim_ref, d_ref, wg_ref, bg_ref,
                      o_ref, hr_ref, hi_ref, xs_ref, un_ref, ub_ref, sre_ref, sim_ref, y_ref, *, n_oct, lc):
    nb, _, d = x_ref.shape
    rows = nb * lc
    ow = d // n_oct
    sw = ar_ref.shape[1] // n_oct

    @pl.when(pl.program_id(0) == 0)
    def _():
        hr_ref[...] = jnp.zeros_like(hr_ref)
        hi_ref[...] = jnp.zeros_like(hi_ref)

    xs_ref[...] = _rms(x_ref[...], g_ref[...])

    def to_time_major(t, c):
        un_ref[pl.ds(pl.multiple_of(t * nb, nb), nb), :] = xs_ref[:, pl.ds(t, 1), :].reshape(nb, d)
        return c

    lax.fori_loop(0, lc, to_time_major, 0, unroll=True)
    ub_ref[...] = un_ref[...].astype(BF16)

    for o in range(n_oct):
        uo = ub_ref[:, o * ow:(o + 1) * ow]
        sre_ref[...] = jnp.dot(uo, bre_ref[o], preferred_element_type=F32)
        sim_ref[...] = jnp.dot(uo, bim_ref[o], preferred_element_type=F32)
        ar = ar_ref[:, o * sw:(o + 1) * sw]
        ai = ai_ref[:, o * sw:(o + 1) * sw]

        def step(t, carry):
            hr, hi = carry
            r = pl.ds(pl.multiple_of(t * nb, nb), nb)
            nr = ar * hr - ai * hi + sre_ref[r, :]
            ni = ar * hi + ai * hr + sim_ref[r, :]
            sre_ref[r, :] = nr
            sim_ref[r, :] = ni
            return nr, ni

        hr, hi = lax.fori_loop(0, lc, step, (hr_ref[:, o * sw:(o + 1) * sw], hi_ref[:, o * sw:(o + 1) * sw]),
                               unroll=True)
        hr_ref[:, o * sw:(o + 1) * sw] = hr
        hi_ref[:, o * sw:(o + 1) * sw] = hi
        y_ref[:, o * ow:(o + 1) * ow] = (
            jnp.dot(sre_ref[...].astype(BF16), cre_ref[o], preferred_element_type=F32)
            - jnp.dot(sim_ref[...].astype(BF16), cim_ref[o], preferred_element_type=F32))

    y_ref[...] = _glu_tail(y_ref[...], un_ref[...], d_ref, wg_ref, bg_ref, d)

    def to_batch_major(t, c):
        mix = y_ref[pl.ds(pl.multiple_of(t * nb, nb), nb), :].reshape(nb, 1, d)
        o_ref[:, pl.ds(t, 1), :] = x_ref[:, pl.ds(t, 1), :] + mix
        return c

    lax.fori_loop(0, lc, to_batch_major, 0, unroll=True)


def _s5_prompt(x, g, prm, d_skip, w_glu, b_glu, *, n_oct):
    nb, t, d = x.shape
    assert nb == SUBLANES
    ar, ai, bre, bim, cre, cim = prm
    ns = ar.shape[1]
    lc = 64
    assert t % lc == 0
    rows = nb * lc
    ar8 = jnp.broadcast_to(ar, (nb, ns))
    ai8 = jnp.broadcast_to(ai, (nb, ns))
    xblk = pl.BlockSpec((nb, lc, d), lambda c: (0, c, 0))
    st = pl.BlockSpec((nb, ns), lambda c: (0, 0))
    need = (4 * nb * lc * d * 4 + 3 * rows * d * 4 + rows * d * 2 + 2 * rows * (ns // n_oct) * 4
            + 2 * (bre.size * 2 + cre.size * 2 + w_glu.size) + rows * 2 * d * 4 + 4 * nb * ns * 4)
    return pl.pallas_call(
        functools.partial(_s5_prompt_kernel, n_oct=n_oct, lc=lc),
        grid=(t // lc,),
        in_specs=[xblk, _const_spec((1, d)), _const_spec((nb, ns)), _const_spec((nb, ns)),
                  _const_spec(bre.shape), _const_spec(bim.shape), _const_spec(cre.shape), _const_spec(cim.shape),
                  _const_spec((1, d)), _const_spec(w_glu.shape), _const_spec((1, 2 * d))],
        out_specs=[xblk, st, st],
        out_shape=[jax.ShapeDtypeStruct((nb, t, d), F32), jax.ShapeDtypeStruct((nb, ns), F32),
                   jax.ShapeDtypeStruct((nb, ns), F32)],
        scratch_shapes=[pltpu.VMEM((nb, lc, d), F32), pltpu.VMEM((rows, d), F32), pltpu.VMEM((rows, d), BF16),
                        pltpu.VMEM((rows, ns // n_oct), F32), pltpu.VMEM((rows, ns // n_oct), F32),
                        pltpu.VMEM((rows, d), F32)],
        compiler_params=pltpu.CompilerParams(dimension_semantics=("arbitrary",),
                                             vmem_limit_bytes=_vmem_limit(need)),
        name="s5_prompt",
    )(x, g.reshape(1, d), ar8, ai8, bre.astype(BF16), bim.astype(BF16), cre.astype(BF16), cim.astype(BF16),
      d_skip.reshape(1, d), w_glu, b_glu.reshape(1, 2 * d))


def _s5_sample_kernel(x_ref, g_ref, h0r_ref, h0i_ref, ar_ref, ai_ref, bre_ref, bim_ref, cre_ref, cim_ref,
                      d_ref, wg_ref, bg_ref, o_ref, hr_ref, hi_ref, y_ref, *, n_oct):
    d = x_ref.shape[1]
    ow = d // n_oct
    sw = ar_ref.shape[1] // n_oct
    x = x_ref[...]
    xn = _rms(x, g_ref[...])
    u = _split3(xn)

    def dot3(lhs3, w):
        w1, w2, w3 = _split3(w)
        l1, l2, l3 = lhs3
        dd = lambda p, q_: jnp.dot(p, q_, preferred_element_type=F32)
        return ((dd(l1, w1) + dd(l1, w2)) + (dd(l2, w1) + dd(l1, w3))) + (dd(l2, w2) + dd(l3, w1))

    for o in range(n_oct):
        uo = tuple(p[:, o * ow:(o + 1) * ow] for p in u)
        ar = ar_ref[:, o * sw:(o + 1) * sw]
        ai = ai_ref[:, o * sw:(o + 1) * sw]
        hr = h0r_ref[:, o * sw:(o + 1) * sw]
        hi = h0i_ref[:, o * sw:(o + 1) * sw]
        nr = ar * hr - ai * hi + dot3(uo, bre_ref[o])
        ni = ar * hi + ai * hr + dot3(uo, bim_ref[o])
        hr_ref[:, o * sw:(o + 1) * sw] = nr
        hi_ref[:, o * sw:(o + 1) * sw] = ni
        y_ref[:, o * ow:(o + 1) * ow] = (
            jnp.dot(nr.astype(BF16), cre_ref[o].astype(BF16), preferred_element_type=F32)
            - jnp.dot(ni.astype(BF16), cim_ref[o].astype(BF16), preferred_element_type=F32))
    o_ref[...] = x + _glu_tail(y_ref[...], xn, d_ref, wg_ref, bg_ref, d)


def _s5_sample(x, g, h_re, h_im, prm, d_skip, w_glu, b_glu, *, n_oct):
    n, d = x.shape
    ar, ai, bre, bim, cre, cim = prm
    ns = ar.shape[1]
    full = lambda shape: pl.BlockSpec(shape, lambda i: (0,) * len(shape))
    need = 4 * (bre.size * 2 + cre.size * 2) + 2 * w_glu.size + 16 * n * ns * 4
    return pl.pallas_call(
        functools.partial(_s5_sample_kernel, n_oct=n_oct),
        grid=(1,),
        in_specs=[full((n, d)), full((1, d)), full((n, ns)), full((n, ns)), full((1, ns)), full((1, ns)),
                  _const_spec(bre.shape), _const_spec(bim.shape), _const_spec(cre.shape), _const_spec(cim.shape),
                  full((1, d)), _const_spec(w_glu.shape), full((1, 2 * d))],
        out_specs=[full((n, d)), full((n, ns)), full((n, ns))],
        out_shape=[jax.ShapeDtypeStruct((n, d), F32), jax.ShapeDtypeStruct((n, ns), F32),
                   jax.ShapeDtypeStruct((n, ns), F32)],
        scratch_shapes=[pltpu.VMEM((n, d), F32)],
        compiler_params=pltpu.CompilerParams(vmem_limit_bytes=_vmem_limit(need)),
        name="s5_sample",
    )(x, g.reshape(1, d), h_re, h_im, ar, ai, bre, bim, cre, cim, d_skip.reshape(1, d), w_glu,
      b_glu.reshape(1, 2 * d))


def kernel(x_prompt, x_sample, cache_k, cache_v, state_ssm_re, state_ssm_im, page_table, rel_bias, norm_g, final_norm_g, ffn_w_in, ffn_w_out, att_w_qkv, att_w_o, ssm_log_dt, ssm_a_re, ssm_a_im, ssm_b_re, ssm_b_im, ssm_c_re, ssm_c_im, ssm_d, ssm_w_glu, ssm_b_glu):
    bp, t, d = x_prompt.shape
    bs, ts, _ = x_sample.shape
    assert ts == 1
    n_heads = cache_k.shape[3]
    hd = cache_k.shape[4]
    page = cache_k.shape[2]
    n_phys = cache_k.shape[1]
    n_pages = page_table.shape[1]
    past_len = n_pages * page
    assert MOBA_BLOCK % page == 0 and past_len % MOBA_BLOCK == 0
    ppb = MOBA_BLOCK // page
    n_full = past_len // MOBA_BLOCK
    assert n_full >= MOBA_TOPK
    n_groups, n_state = ssm_a_re.shape[1], ssm_a_re.shape[2]
    oct_g = LANES // ssm_b_re.shape[-1]
    n_oct = n_groups // oct_g
    depth = norm_g.shape[0]
    assert depth == 2

    w_in = ffn_w_in.astype(BF16)
    w_out = ffn_w_out.astype(BF16)
    w_qkv = att_w_qkv[0].astype(BF16)
    w_o = att_w_o[0].astype(BF16)
    w_glu = ssm_w_glu[0].astype(BF16)

    xp = x_prompt.reshape(bp * t, d)
    xs = x_sample.reshape(bs, d)

    xp = _ffn(xp, norm_g[0, 0], w_in[0, 0], w_out[0, 0], name="ffn_prompt")
    xs = _ffn(xs, norm_g[0, 0], w_in[0, 0], w_out[0, 0], name="ffn_sample")

    tiles, dec = _bias_tables(rel_bias, n_heads)

    kt_p, vt_p, qb, kb, vb, km = _qkv_prompt(xp, norm_g[0, 1], w_qkv, float(hd) ** -0.5, t)
    o_p = _moba_prompt(qb.reshape(bp, t, d), kb.reshape(bp, t, d), vb.reshape(bp, t, d),
                       km.reshape(bp, t // MOBA_BLOCK, d), tiles, rel_bias, n_heads=n_heads)
    xp = _ffn(xp, norm_g[0, 2], w_in[0, 1], w_out[0, 1], pre=(o_p.reshape(bp * t, d), w_o), name="ffn_prompt_o")

    q_s, k_s, v_s = _qkv_sample(xs, norm_g[0, 1], w_qkv)
    ckt = cache_k[0].transpose(0, 2, 3, 1)
    cvt = cache_v[0].transpose(0, 2, 3, 1)
    idx = _decode_gate(page_table, ckt, q_s, n_full=n_full, ppb=ppb)
    idx_flat = idx[:, :, :MOBA_TOPK].reshape(-1)
    o_s = _decode_attn(page_table, idx_flat, ckt, cvt, q_s, k_s, v_s, dec, n_full=n_full, ppb=ppb)
    xs = _ffn(xs, norm_g[0, 2], w_in[0, 1], w_out[0, 1], pre=(o_s.reshape(bs, d).astype(BF16), w_o),
              name="ffn_sample_o")

    xp = _ffn(xp, norm_g[1, 0], w_in[1, 0], w_out[1, 0], name="ffn_prompt_l1")
    xs = _ffn(xs, norm_g[1, 0], w_in[1, 0], w_out[1, 0], name="ffn_sample_l1")
    prm = _s5_params(ssm_log_dt[0], ssm_a_re[0], ssm_a_im[0], ssm_b_re[0], ssm_b_im[0], ssm_c_re[0], ssm_c_im[0],
                     oct_g)
    xp3, hr_p, hi_p = _s5_prompt(xp.reshape(bp, t, d), norm_g[1, 1], prm, ssm_d[0], w_glu, ssm_b_glu[0], n_oct=n_oct)
    xs, hr_s, hi_s = _s5_sample(xs, norm_g[1, 1], state_ssm_re[0].reshape(bs, n_groups * n_state),
                                state_ssm_im[0].reshape(bs, n_groups * n_state), prm, ssm_d[0], w_glu,
                                ssm_b_glu[0], n_oct=n_oct)
    xp = _ffn(xp3.reshape(bp * t, d), norm_g[1, 2], w_in[1, 1], w_out[1, 1], final_g=final_norm_g,
              name="ffn_prompt_final")
    xs = _ffn(xs, norm_g[1, 2], w_in[1, 1], w_out[1, 1], final_g=final_norm_g, name="ffn_sample_final")

    st = lambda a, b: a.reshape(1, b, n_groups, n_state)
    tok_major = lambda a: a.reshape(1, bp, n_heads, hd, t).transpose(0, 1, 4, 2, 3)
    return (xp.reshape(bp, t, d), xs.reshape(bs, 1, d), tok_major(kt_p), tok_major(vt_p),
            k_s.reshape(1, bs, 1, n_heads, hd), v_s.reshape(1, bs, 1, n_heads, hd),
            st(hr_p, bp), st(hi_p, bp), st(hr_s, bs), st(hi_s, bs))
```

```python
import functools
import math

import numpy as np
import jax
import jax.numpy as jnp
from jax import lax
from jax.experimental import pallas as pl
from jax.experimental.pallas import tpu as pltpu

F32 = jnp.float32
BF16 = jnp.bfloat16
I32 = jnp.int32

RMS_EPS = 1e-6
NEG = -1e30
LOG2E = math.log2(math.e)
MOBA_BLOCK = 256
MOBA_TOPK = 3
REL_MAX_DIST = 128
LANES = 128
SUBLANES = 8
VMEM_CAP = 60 * 1024 * 1024

_NT = (((1,), (1,)), ((), ()))


def _vmem_limit(nbytes):
    return int(min(VMEM_CAP, nbytes + (8 << 20)))


def _const_spec(shape):
    nd = len(shape)
    return pl.BlockSpec(shape, lambda *_: (0,) * nd, pipeline_mode=pl.Buffered(1))


def _rms(x, g):
    return x * lax.rsqrt(jnp.mean(x * x, axis=-1, keepdims=True) + RMS_EPS) * g


def _split3(x):
    a = x.astype(BF16)
    r = x - a.astype(F32)
    b = r.astype(BF16)
    c = (r - b.astype(F32)).astype(BF16)
    return a, b, c


def _ffn_kernel(*refs, d_ff, chunk, has_pre, has_final):
    it = iter(refs)
    x_ref = next(it)
    if has_pre:
        a_ref, wpre_ref = next(it), next(it)
    g_ref, win_ref, wout_ref = next(it), next(it), next(it)
    if has_final:
        gf_ref = next(it)
    o_ref, act_ref = next(it), next(it)

    x = x_ref[...]
    if has_pre:
        x = x + jnp.dot(a_ref[...], wpre_ref[...], preferred_element_type=F32)
    xn = _rms(x, g_ref[...]).astype(BF16)
    for c in range(d_ff // chunk):
        lo = c * chunk
        gate = jnp.dot(xn, win_ref[:, lo:lo + chunk], preferred_element_type=F32)
        up = jnp.dot(xn, win_ref[:, d_ff + lo:d_ff + lo + chunk], preferred_element_type=F32)
        act_ref[:, lo:lo + chunk] = (gate * jax.nn.sigmoid(gate) * up).astype(BF16)
    y = x + 0.5 * jnp.dot(act_ref[...], wout_ref[...], preferred_element_type=F32)
    if has_final:
        y = _rms(y, gf_ref[...])
    o_ref[...] = y


def _ffn(x, g, w_in, w_out, *, pre=None, final_g=None, name):
    n, d = x.shape
    d_ff = w_out.shape[0]
    tm = min(n, 512)
    chunk = 256
    assert n % tm == 0 and d_ff % chunk == 0
    row = lambda i: (i, 0)
    args, specs = [x], [pl.BlockSpec((tm, d), row)]
    if pre is not None:
        a, w_pre = pre
        args += [a, w_pre]
        specs += [pl.BlockSpec((tm, a.shape[1]), row), _const_spec(w_pre.shape)]
    args += [g.reshape(1, d), w_in, w_out]
    specs += [_const_spec((1, d)), _const_spec(w_in.shape), _const_spec(w_out.shape)]
    if final_g is not None:
        args.append(final_g.reshape(1, d))
        specs.append(_const_spec((1, d)))
    wbytes = 2 * (w_in.size + w_out.size + (pre[1].size if pre is not None else 0))
    need = wbytes + tm * d * 4 * 4 + tm * d_ff * 2 + tm * chunk * 4 * 6 + tm * d * 4 * 3
    return pl.pallas_call(
        functools.partial(_ffn_kernel, d_ff=d_ff, chunk=chunk, has_pre=pre is not None,
                          has_final=final_g is not None),
        grid=(n // tm,),
        in_specs=specs,
        out_specs=pl.BlockSpec((tm, d), row),
        out_shape=jax.ShapeDtypeStruct((n, d), F32),
        scratch_shapes=[pltpu.VMEM((tm, d_ff), BF16)],
        compiler_params=pltpu.CompilerParams(dimension_semantics=("parallel",),
                                             vmem_limit_bytes=_vmem_limit(need)),
        name=name,
    )(*args)


def _qkv_prompt_kernel(x_ref, g_ref, w_ref, kt_ref, vt_ref, qb_ref, kb_ref, vb_ref, km_ref, *, d, scale):
    xn = _rms(x_ref[...], g_ref[...]).astype(BF16)
    q = jnp.dot(xn, w_ref[:, 0:d], preferred_element_type=F32)
    qb_ref[...] = (q * scale).astype(BF16)
    k = jnp.dot(xn, w_ref[:, d:2 * d], preferred_element_type=F32)
    kt_ref[0] = k.T
    kb_ref[...] = k.astype(BF16)
    for r in range(k.shape[0] // MOBA_BLOCK):
        km_ref[r] = jnp.sum(k[r * MOBA_BLOCK:(r + 1) * MOBA_BLOCK], axis=0, keepdims=True) * (1.0 / MOBA_BLOCK)
    v = jnp.dot(xn, w_ref[:, 2 * d:3 * d], preferred_element_type=F32)
    vt_ref[0] = v.T
    vb_ref[...] = v.astype(BF16)


def _qkv_prompt(x, g, w_qkv, scale, t):
    n, d = x.shape
    tm = 512
    assert n % t == 0 and t % tm == 0 and tm % MOBA_BLOCK == 0
    tpb = t // tm
    row = lambda i: (i, 0)
    rs = pl.BlockSpec((tm, d), row)
    ts = pl.BlockSpec((1, d, tm), lambda i: (i // tpb, 0, i % tpb))
    need = 2 * w_qkv.size + tm * d * (4 * 2 * 3 + 2 * 2 * 3) + tm * d * 4 * 6
    return pl.pallas_call(
        functools.partial(_qkv_prompt_kernel, d=d, scale=scale),
        grid=(n // tm,),
        in_specs=[rs, _const_spec((1, d)), _const_spec(w_qkv.shape)],
        out_specs=[ts, ts, rs, rs, rs,
                   pl.BlockSpec((tm // MOBA_BLOCK, 1, d), lambda i: (i, 0, 0))],
        out_shape=[jax.ShapeDtypeStruct((n // t, d, t), F32), jax.ShapeDtypeStruct((n // t, d, t), F32),
                   jax.ShapeDtypeStruct((n, d), BF16), jax.ShapeDtypeStruct((n, d), BF16),
                   jax.ShapeDtypeStruct((n, d), BF16),
                   jax.ShapeDtypeStruct((n // MOBA_BLOCK, 1, d), F32)],
        compiler_params=pltpu.CompilerParams(dimension_semantics=("parallel",),
                                             vmem_limit_bytes=_vmem_limit(need)),
        name="qkv_prompt",
    )(x, g.reshape(1, d), w_qkv)


def _qkv_sample_kernel(x_ref, g_ref, w_ref, q_ref, k_ref, v_ref, *, d):
    xn = _rms(x_ref[...], g_ref[...]).astype(BF16)
    q_ref[...] = jnp.dot(xn, w_ref[:, 0:d], preferred_element_type=F32)
    k_ref[...] = jnp.dot(xn, w_ref[:, d:2 * d], preferred_element_type=F32)
    v_ref[...] = jnp.dot(xn, w_ref[:, 2 * d:3 * d], preferred_element_type=F32)


def _qkv_sample(x, g, w_qkv):
    n, d = x.shape
    full = pl.BlockSpec((n, d), lambda i: (0, 0))
    sd = jax.ShapeDtypeStruct((n, d), F32)
    return pl.pallas_call(
        functools.partial(_qkv_sample_kernel, d=d),
        grid=(1,),
        in_specs=[full, _const_spec((1, d)), _const_spec(w_qkv.shape)],
        out_specs=[full, full, full],
        out_shape=[sd, sd, sd],
        compiler_params=pltpu.CompilerParams(vmem_limit_bytes=_vmem_limit(2 * w_qkv.size + 16 * n * d)),
        name="qkv_sample",
    )(x, g.reshape(1, d), w_qkv)


def _rel_bucket_np(n, n_buckets):
    n = np.maximum(n, 0)
    max_exact = n_buckets // 2
    nf = np.maximum(n, max_exact).astype(np.float32)
    large = max_exact + (np.log(nf / np.float32(max_exact)) / np.float32(math.log(REL_MAX_DIST / max_exact))
                         * np.float32(n_buckets - max_exact)).astype(np.int32)
    large = np.minimum(large, n_buckets - 1)
    return np.where(n < max_exact, n, large).astype(np.int32)


def _bias_kernel(rel_ref, bkt_ref, dbk_ref, tile_ref, dec_ref, *, n_buckets):
    h = pl.program_id(0)
    for t in range(2):
        b = bkt_ref[t]
        acc = jnp.full(b.shape, NEG, F32)
        for u in range(n_buckets):
            acc = jnp.where(b == u, rel_ref[u, h], acc)
        tile_ref[0, t] = acc * LOG2E
    b = dbk_ref[...]
    acc = jnp.zeros(b.shape, F32)
    for u in range(n_buckets):
        acc = jnp.where(b == u, rel_ref[u, h], acc)
    dec_ref[0] = acc


def _bias_tables(rel_bias, n_heads):
    n_buckets = rel_bias.shape[0]
    blk = MOBA_BLOCK
    qi = np.arange(blk)[:, None]
    ki = np.arange(blk)[None, :]
    own = np.where(ki <= qi, _rel_bucket_np(qi - ki, n_buckets), -1)
    adj = _rel_bucket_np(blk + qi - ki, n_buckets)
    bkt = np.stack([own, adj]).astype(np.int32)
    dbk = np.zeros((SUBLANES, blk), np.int32)
    dbk[0] = _rel_bucket_np(blk - np.arange(blk), n_buckets)
    dbk[1] = _rel_bucket_np(np.full((blk,), 2 * blk), n_buckets)
    dbk[2] = 0
    return pl.pallas_call(
        functools.partial(_bias_kernel, n_buckets=n_buckets),
        grid=(n_heads,),
        in_specs=[pl.BlockSpec(memory_space=pltpu.SMEM),
                  pl.BlockSpec((2, blk, blk), lambda h: (0, 0, 0)),
                  pl.BlockSpec((SUBLANES, blk), lambda h: (0, 0))],
        out_specs=[pl.BlockSpec((1, 2, blk, blk), lambda h: (h, 0, 0, 0)),
                   pl.BlockSpec((1, SUBLANES, blk), lambda h: (h, 0, 0))],
        out_shape=[jax.ShapeDtypeStruct((n_heads, 2, blk, blk), F32),
                   jax.ShapeDtypeStruct((n_heads, SUBLANES, blk), F32)],
        name="rel_bias_tables",
    )(rel_bias, jnp.asarray(bkt), jnp.asarray(dbk))


def _moba_kernel(pt_ref, rel_ref, q_ref, k_ref, v_ref, km_ref, tile_ref, *rest, n_blk, hd, far_bucket, gate):
    n_gate_pages = gate["bps"] * gate["ppb"]
    gate_pages = rest[0:n_gate_pages]
    qs_ref, o_ref, idx_ref, q2_ref, s_ref, m_ref, acc_ref, qb_ref, g_ref = rest[n_gate_pages:]
    blk = MOBA_BLOCK
    half = 2 * blk
    rows = 2 * half
    dead = 2 * n_blk
    hp = pl.program_id(1)
    p = pl.program_id(2)
    lane = lax.broadcasted_iota(I32, (blk, LANES), 1)

    step = (pl.program_id(0) * pl.num_programs(1) + hp) * pl.num_programs(2) + p
    chunk = step % gate["chunks"]
    _decode_gate_step(gate_pages, qs_ref, qb_ref, g_ref, chunk, ppb=gate["ppb"], bps=gate["bps"],
                      n_heads=gate["n_heads"], hd=gate["hd"])

    def two_heads(q):
        return [jnp.where(lane < hd, q, jnp.zeros_like(q)), jnp.where(lane >= hd, q, jnp.zeros_like(q))]

    q2 = jnp.concatenate(two_heads(q_ref[0, 0:blk, :]) + two_heads(q_ref[0, blk:2 * blk, :]), axis=0)
    km = km_ref[0]
    km_hi = km.astype(BF16)
    km_lo = (km - km_hi.astype(F32)).astype(BF16)
    gt = (lax.dot_general(km_hi, q2, _NT, preferred_element_type=F32)
          + lax.dot_general(km_lo, q2, _NT, preferred_element_type=F32))
    bidx = lax.broadcasted_iota(I32, (n_blk, rows), 0)
    qcol = lax.broadcasted_iota(I32, (n_blk, rows), 1)
    i_col = 2 * p + jnp.where(qcol >= half, 1, 0)
    valid = bidx < i_col
    gm = jnp.where(valid, gt, NEG)
    rank = jnp.zeros((n_blk, rows), I32)
    for jp in range(n_blk):
        row = gm[jp:jp + 1, :]
        rank = rank + jnp.where(row > gm, 1, jnp.where(row == gm, jnp.where(bidx > jp, 1, 0), 0))
    far = jnp.where((qcol & blk) == 0, rel_ref[far_bucket, 2 * hp], rel_ref[far_bucket, 2 * hp + 1]) * LOG2E
    chosen = jnp.where(rank < MOBA_TOPK, jnp.where(bidx <= i_col - 2, far, 0.0), NEG)
    aval = jnp.where(valid, chosen, jnp.where(bidx == i_col, 0.0, NEG))
    a_hi = aval.astype(BF16).astype(F32)
    a_lo = (aval - a_hi).astype(BF16).astype(F32)
    dead_rows = jnp.where(lax.broadcasted_iota(I32, (SUBLANES, rows), 0) == 0, NEG, 0.0)
    at = jnp.concatenate([a_hi, a_lo, dead_rows, jnp.zeros((LANES - dead - SUBLANES, rows), F32)], axis=0)
    q2_ref[:, 0:LANES] = q2
    q2_ref[:, LANES:2 * LANES] = at.T.astype(BF16)

    ones_col = jnp.where(lane == 0, 1.0, 0.0).astype(BF16)
    n_far = jnp.maximum(2 * p - 1, 0)

    def scores(j, masked):
        kj = k_ref[0, pl.ds(pl.multiple_of(j * blk, blk), blk), :]
        c_hi = jnp.where(masked, dead, j)
        c_lo = jnp.where(masked, dead, n_blk + j)
        ej = jnp.where((lane == c_hi) | (lane == c_lo), 1.0, 0.0).astype(BF16)
        return lax.dot_general(q2_ref[...], jnp.concatenate([kj, ej], axis=1), _NT, preferred_element_type=F32)

    def v_aug(j):
        vj = v_ref[0, pl.ds(pl.multiple_of(j * blk, blk), blk), :]
        return jnp.concatenate([vj, ones_col], axis=1)

    def lane_max(s):
        return jnp.maximum(s[:, 0:LANES], s[:, LANES:2 * LANES])

    def far_id(j):
        pad = j >= n_far
        return jnp.where(pad, n_blk - 1, j), pad

    t_own = jnp.concatenate([tile_ref[0, 0], tile_ref[1, 0]], axis=0)
    t_adj = jnp.concatenate([tile_ref[0, 1], tile_ref[1, 1]], axis=0)
    s0 = scores(jnp.maximum(2 * p - 1, 0), p == 0)
    s1 = scores(2 * p, False)
    s2 = scores(2 * p + 1, False)
    s0a = s0[0:half] + t_adj
    s1a = s1[0:half] + t_own
    s1b = s1[half:rows] + t_adj
    s2b = s2[half:rows] + t_own
    s_ref[0, 0:half] = s0a
    s_ref[0, half:rows] = s0[half:rows]
    s_ref[1, 0:half] = s1a
    s_ref[1, half:rows] = s1b
    s_ref[2, 0:half] = s2[0:half]
    s_ref[2, half:rows] = s2b
    m_ref[0:half] = jnp.maximum(jnp.maximum(lane_max(s0a), lane_max(s1a)), lane_max(s2[0:half]))
    m_ref[half:rows] = jnp.maximum(jnp.maximum(lane_max(s0[half:rows]), lane_max(s1b)), lane_max(s2b))

    n4 = (n_far + 1) // 4
    tail2 = n_far - 4 * n4 == 1

    def a_group(base, width):
        mm = m_ref[...]
        for u in range(width):
            j = base + u
            s = scores(*far_id(j))
            s_ref[3 + j] = s
            mm = jnp.maximum(mm, lane_max(s))
        m_ref[...] = mm

    def a_trip(it, c):
        a_group(it * 4, 4)
        return c

    lax.fori_loop(0, n4, a_trip, 0)

    @pl.when(tail2)
    def _():
        a_group(4 * n4, 2)

    m_ref[...] = jnp.broadcast_to(jnp.max(m_ref[...], axis=-1, keepdims=True), m_ref.shape)

    def prob(slot):
        mb = m_ref[...]
        return jnp.exp2(s_ref[slot] - jnp.concatenate([mb, mb], axis=1)).astype(BF16)

    acc_ref[...] = ((jnp.dot(prob(0), v_aug(jnp.maximum(2 * p - 1, 0)), preferred_element_type=F32)
                     + jnp.dot(prob(1), v_aug(2 * p), preferred_element_type=F32))
                    + jnp.dot(prob(2), v_aug(2 * p + 1), preferred_element_type=F32))

    def b_group(base, width):
        part = None
        for u in range(width):
            j = base + u
            d_ = jnp.dot(prob(3 + j), v_aug(far_id(j)[0]), preferred_element_type=F32)
            part = d_ if part is None else part + d_
        acc_ref[...] += part

    def b_trip(it, c):
        b_group(it * 4, 4)
        return c

    lax.fori_loop(0, n4, b_trip, 0)

    @pl.when(tail2)
    def _():
        b_group(4 * n4, 2)

    acc = acc_ref[...]
    for qb in range(2):
        r0 = qb * half
        o0 = acc[r0:r0 + blk, 0:LANES] / acc[r0:r0 + blk, LANES:LANES + 1]
        o1 = acc[r0 + blk:r0 + half, 0:LANES] / acc[r0 + blk:r0 + half, LANES:LANES + 1]
        o_ref[0, qb * blk:(qb + 1) * blk, :] = jnp.where(lane < hd, o0, o1).astype(BF16)

    _decode_gate_topk(idx_ref, g_ref, chunk, gate["chunks"], n_full=gate["n_full"])


def _moba_prompt(qb, kb, vb, km, tiles, rel_bias, page_table, cache_kt, q_s, *, n_heads, n_full, ppb):
    b, t, d = qb.shape
    hd = d // n_heads
    assert 2 * hd == LANES and t % MOBA_BLOCK == 0
    n_blk = t // MOBA_BLOCK
    assert 2 * n_blk + SUBLANES <= LANES and n_blk % 2 == 0
    blk = MOBA_BLOCK
    rows = 4 * blk
    far_bucket = int(_rel_bucket_np(np.array([2 * blk]), rel_bias.shape[0])[0])
    assert int(_rel_bucket_np(np.array([blk + 1]), rel_bias.shape[0])[0]) == far_bucket
    n_hp, n_p = n_heads // 2, n_blk // 2

    ns, ds = q_s.shape
    n_phys, heads_s, hd_s, page = cache_kt.shape
    assert page == LANES and n_full <= LANES
    n_pages = page_table.shape[1]
    n_steps = b * n_hp * n_p
    assert (ns * n_full) % n_steps == 0
    bps = ns * n_full // n_steps
    assert n_full % bps == 0
    chunks = n_full // bps
    gate = dict(n_full=n_full, ppb=ppb, bps=bps, n_heads=heads_s, hd=hd_s, chunks=chunks)

    def step_of(bi, hp, p):
        return (bi * n_hp + hp) * n_p + p

    def page_spec(bb, r):
        def imap(bi, hp, p, pt):
            st = step_of(bi, hp, p)
            return (pt[(st // chunks) * n_pages + ((st % chunks) * bps + bb) * ppb + r], 0, 0, 0)
        return pl.BlockSpec((1, heads_s, hd_s, page), imap)

    seq = pl.BlockSpec((1, t, LANES), lambda bi, hp, p, pt: (bi, 0, hp))
    qblk = pl.BlockSpec((1, 2 * blk, LANES), lambda bi, hp, p, pt: (bi, p, hp))
    per_seq = lambda width: pl.BlockSpec((1, width[0], width[1]),
                                         lambda bi, hp, p, pt: (step_of(bi, hp, p) // chunks, 0, 0))
    n_slots = 3 + n_blk
    need = (2 * 2 * t * LANES * 2 + n_slots * rows * blk * 4 + 2 * 4 * blk * blk * 4
            + rows * (2 * LANES * 2 + LANES * 4 + 2 * LANES * 4)
            + 2 * bps * ppb * ds * page * 4 + ds * LANES * 4 + (8 << 20))
    grid_spec = pltpu.PrefetchScalarGridSpec(
        num_scalar_prefetch=1,
        grid=(b, n_hp, n_p),
        in_specs=[pl.BlockSpec(memory_space=pltpu.SMEM), qblk, seq, seq,
                  pl.BlockSpec((1, n_blk, LANES), lambda bi, hp, p, pt: (bi, 0, hp)),
                  pl.BlockSpec((2, 2, blk, blk), lambda bi, hp, p, pt: (hp, 0, 0, 0))]
                 + [page_spec(bb, r) for bb in range(bps) for r in range(ppb)]
                 + [per_seq((1, ds))],
        out_specs=[qblk, per_seq((heads_s, LANES))],
        scratch_shapes=[pltpu.VMEM((rows, 2 * LANES), BF16), pltpu.VMEM((n_slots, rows, blk), F32),
                        pltpu.VMEM((rows, LANES), F32), pltpu.VMEM((rows, 2 * LANES), F32),
                        pltpu.VMEM((ds, LANES), F32), pltpu.VMEM((heads_s, LANES), F32)],
    )
    return pl.pallas_call(
        functools.partial(_moba_kernel, n_blk=n_blk, hd=hd, far_bucket=far_bucket, gate=gate),
        grid_spec=grid_spec,
        out_shape=[jax.ShapeDtypeStruct((b, t, d), BF16), jax.ShapeDtypeStruct((ns, heads_s, LANES), I32)],
        compiler_params=pltpu.CompilerParams(dimension_semantics=("arbitrary", "arbitrary", "arbitrary"),
                                             vmem_limit_bytes=_vmem_limit(need)),
        name="moba_prompt",
    )(page_table.reshape(-1), rel_bias, qb, kb, vb, km, tiles, *([cache_kt] * (bps * ppb)),
      q_s.reshape(ns, 1, ds))


def _decode_gate_step(pages, q_ref, qb_ref, g_ref, jj, *, ppb, bps, n_heads, hd):
    lane = lax.broadcasted_iota(I32, (n_heads, LANES), 1)

    @pl.when(jj == 0)
    def _():
        q128 = jnp.broadcast_to(q_ref[0], (LANES, n_heads * hd))
        for c in range(n_heads * hd // LANES):
            qb_ref[c * LANES:(c + 1) * LANES, :] = q128[:, c * LANES:(c + 1) * LANES].T
        g_ref[...] = jnp.zeros_like(g_ref)

    head = lax.broadcasted_iota(I32, (n_heads, LANES), 0)
    g = g_ref[...]
    for h in range(n_heads):
        accs = [None] * bps
        for c in range(hd // SUBLANES):
            rows = slice(c * SUBLANES, (c + 1) * SUBLANES)
            qv = qb_ref[h * hd + c * SUBLANES:h * hd + (c + 1) * SUBLANES, :]
            for bb in range(bps):
                kk = pages[bb * ppb][0, h, rows, :]
                for r in range(1, ppb):
                    kk = kk + pages[bb * ppb + r][0, h, rows, :]
                accs[bb] = kk * qv if accs[bb] is None else accs[bb] + kk * qv
        for bb in range(bps):
            tot = jnp.sum(jnp.sum(accs[bb], axis=0, keepdims=True), axis=-1, keepdims=True) * (1.0 / MOBA_BLOCK)
            g = jnp.where(lane == jj * bps + bb, jnp.where(head == h, tot, g), g)
    g_ref[...] = g


def _decode_gate_topk(idx_ref, g_ref, jj, n_chunks, *, n_full):
    n_heads = g_ref.shape[0]
    lane = lax.broadcasted_iota(I32, (n_heads, LANES), 1)

    @pl.when(jj == n_chunks - 1)
    def _():
        g = jnp.where(lane < n_full, g_ref[...], -3e38)
        out = jnp.zeros((n_heads, LANES), I32)
        for s in range(MOBA_TOPK):
            mx = jnp.max(g, axis=-1, keepdims=True)
            ix = jnp.min(jnp.where(g == mx, lane, LANES), axis=-1, keepdims=True)
            out = jnp.where(lane == s, ix, out)
            g = jnp.where(lane == ix, -3e38, g)
        idx_ref[0] = out


def _decode_attn_kernel(pt_ref, idx_ref, *refs, n_slots, ppb, n_full, hps, hd, scale):
    n_pg = hps * n_slots * ppb
    kp = refs[0:n_pg]
    vp = refs[n_pg:2 * n_pg]
    q_ref, kn_ref, vn_ref, dec_ref, o_ref = refs[2 * n_pg:]
    bg = pl.program_id(0) * pl.num_programs(1) + pl.program_id(1)
    page = kp[0].shape[-1]
    outs = []
    for hh in range(hps):
        cols = slice(hh * hd, (hh + 1) * hd)
        q = q_ref[0][:, cols]
        q8 = jnp.broadcast_to(q, (SUBLANES, hd)).astype(BF16)
        dec = dec_ref[hh]
        s_list, v_list = [], []
        for s in range(n_slots):
            blk_id = idx_ref[(bg * hps + hh) * n_slots + s]
            bias = jnp.where(blk_id == n_full - 1, dec[0:1, :], dec[1:2, :])
            for r in range(ppb):
                pg = (hh * n_slots + s) * ppb + r
                kt = kp[pg][0, 0].astype(BF16)
                sc = jnp.dot(q8, kt, preferred_element_type=F32) * scale + bias[:, r * page:(r + 1) * page]
                s_list.append(sc)
                v_list.append(vp[pg][0, 0].astype(BF16))
        s_self = jnp.sum(q * kn_ref[0][:, cols], axis=-1, keepdims=True) * scale + dec[2:3, 0:1]
        m = s_self
        for sc in s_list:
            m = jnp.maximum(m, jnp.max(sc, axis=-1, keepdims=True))
        p_self = jnp.exp(s_self - m)
        l = p_self
        acc = p_self * vn_ref[0][:, cols]
        for sc, vt in zip(s_list, v_list):
            p = jnp.exp(sc - m)
            l = l + jnp.sum(p, axis=-1, keepdims=True)
            acc = acc + lax.dot_general(p.astype(BF16), vt, _NT, preferred_element_type=F32)
        outs.append((acc / l)[0:1, :])
    o_ref[0] = jnp.concatenate(outs, axis=1)


def _decode_attn(page_table, idx_flat, cache_kt, cache_vt, q, k_new, v_new, dec, *, n_full, ppb):
    nb, d = q.shape
    n_phys, n_heads, hd, page = cache_kt.shape
    n_pages = page_table.shape[1]
    n_slots = MOBA_TOPK
    hps = 2
    assert n_heads % hps == 0
    n_grp = n_heads // hps

    def page_spec(hh, s, r):
        def imap(b, g, pt, idx):
            h = g * hps + hh
            blk_id = idx[(b * n_heads + h) * n_slots + s]
            return (pt[b * n_pages + blk_id * ppb + r], h, 0, 0)
        return pl.BlockSpec((1, 1, hd, page), imap)

    grp = pl.BlockSpec((1, 1, hps * hd), lambda b, g, pt, idx: (b * n_grp + g, 0, 0))
    pspecs = [page_spec(hh, s, r) for hh in range(hps) for s in range(n_slots) for r in range(ppb)]
    grid_spec = pltpu.PrefetchScalarGridSpec(
        num_scalar_prefetch=2,
        grid=(nb, n_grp),
        in_specs=pspecs + pspecs + [grp, grp, grp,
                                    pl.BlockSpec((hps, SUBLANES, MOBA_BLOCK), lambda b, g, pt, idx: (g, 0, 0))],
        out_specs=grp,
    )
    n_pg = hps * n_slots * ppb
    per_grp = lambda a: a.reshape(nb * n_grp, 1, hps * hd)
    return pl.pallas_call(
        functools.partial(_decode_attn_kernel, n_slots=n_slots, ppb=ppb, n_full=n_full, hps=hps, hd=hd,
                          scale=float(hd) ** -0.5),
        grid_spec=grid_spec,
        out_shape=jax.ShapeDtypeStruct((nb * n_grp, 1, hps * hd), F32),
        compiler_params=pltpu.CompilerParams(dimension_semantics=("arbitrary", "arbitrary")),
        name="decode_attn",
    )(page_table.reshape(-1), idx_flat, *([cache_kt] * n_pg), *([cache_vt] * n_pg),
      per_grp(q), per_grp(k_new), per_grp(v_new), dec)


def _s5_params(log_dt, a_re, a_im, b_re, b_im, c_re, c_im, oct_g):
    g, p = a_re.shape
    gd = b_re.shape[-1]
    dt = jnp.exp(log_dt.astype(F32))[:, None]
    ar, ai = a_re.astype(F32), a_im.astype(F32)
    mag = jnp.exp(dt * ar)
    abar_re, abar_im = mag * jnp.cos(dt * ai), mag * jnp.sin(dt * ai)
    den = ar * ar + ai * ai
    coef_re = ((abar_re - 1.0) * ar + abar_im * ai) / den
    coef_im = (abar_im * ar - (abar_re - 1.0) * ai) / den
    br, bim = b_re.astype(F32), b_im.astype(F32)
    bbar_re = coef_re[..., None] * br - coef_im[..., None] * bim
    bbar_im = coef_re[..., None] * bim + coef_im[..., None] * br
    n_oct = g // oct_g
    eye = jnp.eye(oct_g, dtype=F32)

    def in_blocks(bb):
        bb = bb.reshape(n_oct, oct_g, p, gd).transpose(0, 1, 3, 2)
        return jnp.einsum('ogip,gh->ogihp', bb, eye).reshape(n_oct, oct_g * gd, oct_g * p)

    def out_blocks(cc):
        cc = cc.reshape(n_oct, oct_g, gd, p).transpose(0, 1, 3, 2)
        return jnp.einsum('ogpi,gh->ogphi', cc, eye).reshape(n_oct, oct_g * p, oct_g * gd)

    return (abar_re.reshape(1, g * p), abar_im.reshape(1, g * p), in_blocks(bbar_re), in_blocks(bbar_im),
            out_blocks(c_re.astype(F32)), out_blocks(c_im.astype(F32)))


def _glu_tail(y, xn, d_ref, wg_ref, bg_ref, d):
    z = jax.nn.gelu(y + d_ref[...] * xn).astype(BF16)
    gl = jnp.dot(z, wg_ref[...], preferred_element_type=F32) + bg_ref[...]
    return gl[:, 0:d] * jax.nn.sigmoid(gl[:, d:2 * d])


def _s5_prompt_kernel(x_ref, g_ref, ar_ref, ai_ref, bre_ref, bim_ref, cre_ref, cim_ref, d_ref, wg_ref, bg_ref,
                      o_ref, hr_ref, hi_ref, xs_ref, un_ref, ub_ref, sre_ref, sim_ref, y_ref, *, n_oct, lc):
    nb, _, d = x_ref.shape
    rows = nb * lc
    ow = d // n_oct
    sw = ar_ref.shape[1] // n_oct

    @pl.when(pl.program_id(0) == 0)
    def _():
        hr_ref[...] = jnp.zeros_like(hr_ref)
        hi_ref[...] = jnp.zeros_like(hi_ref)

    xs_ref[...] = _rms(x_ref[...], g_ref[...])

    def to_time_major(t, c):
        un_ref[pl.ds(pl.multiple_of(t * nb, nb), nb), :] = xs_ref[:, pl.ds(t, 1), :].reshape(nb, d)
        return c

    lax.fori_loop(0, lc, to_time_major, 0, unroll=True)
    ub_ref[...] = un_ref[...].astype(BF16)

    for o in range(n_oct):
        uo = ub_ref[:, o * ow:(o + 1) * ow]
        sre_ref[...] = jnp.dot(uo, bre_ref[o], preferred_element_type=F32)
        sim_ref[...] = jnp.dot(uo, bim_ref[o], preferred_element_type=F32)
        ar = ar_ref[:, o * sw:(o + 1) * sw]
        ai = ai_ref[:, o * sw:(o + 1) * sw]

        def step(t, carry):
            hr, hi = carry
            r = pl.ds(pl.multiple_of(t * nb, nb), nb)
            nr = ar * hr - ai * hi + sre_ref[r, :]
            ni = ar * hi + ai * hr + sim_ref[r, :]
            sre_ref[r, :] = nr
            sim_ref[r, :] = ni
            return nr, ni

        hr, hi = lax.fori_loop(0, lc, step, (hr_ref[:, o * sw:(o + 1) * sw], hi_ref[:, o * sw:(o + 1) * sw]),
                               unroll=True)
        hr_ref[:, o * sw:(o + 1) * sw] = hr
        hi_ref[:, o * sw:(o + 1) * sw] = hi
        y_ref[:, o * ow:(o + 1) * ow] = (
            jnp.dot(sre_ref[...].astype(BF16), cre_ref[o], preferred_element_type=F32)
            - jnp.dot(sim_ref[...].astype(BF16), cim_ref[o], preferred_element_type=F32))

    y_ref[...] = _glu_tail(y_ref[...], un_ref[...], d_ref, wg_ref, bg_ref, d)

    def to_batch_major(t, c):
        mix = y_ref[pl.ds(pl.multiple_of(t * nb, nb), nb), :].reshape(nb, 1, d)
        o_ref[:, pl.ds(t, 1), :] = x_ref[:, pl.ds(t, 1), :] + mix
        return c

    lax.fori_loop(0, lc, to_batch_major, 0, unroll=True)


def _s5_prompt(x, g, prm, d_skip, w_glu, b_glu, *, n_oct):
    nb, t, d = x.shape
    assert nb == SUBLANES
    ar, ai, bre, bim, cre, cim = prm
    ns = ar.shape[1]
    lc = 64
    assert t % lc == 0
    rows = nb * lc
    ar8 = jnp.broadcast_to(ar, (nb, ns))
    ai8 = jnp.broadcast_to(ai, (nb, ns))
    xblk = pl.BlockSpec((nb, lc, d), lambda c: (0, c, 0))
    st = pl.BlockSpec((nb, ns), lambda c: (0, 0))
    need = (4 * nb * lc * d * 4 + 3 * rows * d * 4 + rows * d * 2 + 2 * rows * (ns // n_oct) * 4
            + 2 * (bre.size * 2 + cre.size * 2 + w_glu.size) + rows * 2 * d * 4 + 4 * nb * ns * 4)
    return pl.pallas_call(
        functools.partial(_s5_prompt_kernel, n_oct=n_oct, lc=lc),
        grid=(t // lc,),
        in_specs=[xblk, _const_spec((1, d)), _const_spec((nb, ns)), _const_spec((nb, ns)),
                  _const_spec(bre.shape), _const_spec(bim.shape), _const_spec(cre.shape), _const_spec(cim.shape),
                  _const_spec((1, d)), _const_spec(w_glu.shape), _const_spec((1, 2 * d))],
        out_specs=[xblk, st, st],
        out_shape=[jax.ShapeDtypeStruct((nb, t, d), F32), jax.ShapeDtypeStruct((nb, ns), F32),
                   jax.ShapeDtypeStruct((nb, ns), F32)],
        scratch_shapes=[pltpu.VMEM((nb, lc, d), F32), pltpu.VMEM((rows, d), F32), pltpu.VMEM((rows, d), BF16),
                        pltpu.VMEM((rows, ns // n_oct), F32), pltpu.VMEM((rows, ns // n_oct), F32),
                        pltpu.VMEM((rows, d), F32)],
        compiler_params=pltpu.CompilerParams(dimension_semantics=("arbitrary",),
                                             vmem_limit_bytes=_vmem_limit(need)),
        name="s5_prompt",
    )(x, g.reshape(1, d), ar8, ai8, bre.astype(BF16), bim.astype(BF16), cre.astype(BF16), cim.astype(BF16),
      d_skip.reshape(1, d), w_glu, b_glu.reshape(1, 2 * d))


def _s5_sample_kernel(x_ref, g_ref, h0r_ref, h0i_ref, ar_ref, ai_ref, bre_ref, bim_ref, cre_ref, cim_ref,
                      d_ref, wg_ref, bg_ref, o_ref, hr_ref, hi_ref, y_ref, *, n_oct):
    d = x_ref.shape[1]
    ow = d // n_oct
    sw = ar_ref.shape[1] // n_oct
    x = x_ref[...]
    xn = _rms(x, g_ref[...])
    u = _split3(xn)

    def dot3(lhs3, w):
        w1, w2, w3 = _split3(w)
        l1, l2, l3 = lhs3
        dd = lambda p, q_: jnp.dot(p, q_, preferred_element_type=F32)
        return ((dd(l1, w1) + dd(l1, w2)) + (dd(l2, w1) + dd(l1, w3))) + (dd(l2, w2) + dd(l3, w1))

    for o in range(n_oct):
        uo = tuple(p[:, o * ow:(o + 1) * ow] for p in u)
        ar = ar_ref[:, o * sw:(o + 1) * sw]
        ai = ai_ref[:, o * sw:(o + 1) * sw]
        hr = h0r_ref[:, o * sw:(o + 1) * sw]
        hi = h0i_ref[:, o * sw:(o + 1) * sw]
        nr = ar * hr - ai * hi + dot3(uo, bre_ref[o])
        ni = ar * hi + ai * hr + dot3(uo, bim_ref[o])
        hr_ref[:, o * sw:(o + 1) * sw] = nr
        hi_ref[:, o * sw:(o + 1) * sw] = ni
        y_ref[:, o * ow:(o + 1) * ow] = (
            jnp.dot(nr.astype(BF16), cre_ref[o].astype(BF16), preferred_element_type=F32)
            - jnp.dot(ni.astype(BF16), cim_ref[o].astype(BF16), preferred_element_type=F32))
    o_ref[...] = x + _glu_tail(y_ref[...], xn, d_ref, wg_ref, bg_ref, d)


def _s5_sample(x, g, h_re, h_im, prm, d_skip, w_glu, b_glu, *, n_oct):
    n, d = x.shape
    ar, ai, bre, bim, cre, cim = prm
    ns = ar.shape[1]
    full = lambda shape: pl.BlockSpec(shape, lambda i: (0,) * len(shape))
    need = 4 * (bre.size * 2 + cre.size * 2) + 2 * w_glu.size + 16 * n * ns * 4
    return pl.pallas_call(
        functools.partial(_s5_sample_kernel, n_oct=n_oct),
        grid=(1,),
        in_specs=[full((n, d)), full((1, d)), full((n, ns)), full((n, ns)), full((1, ns)), full((1, ns)),
                  _const_spec(bre.shape), _const_spec(bim.shape), _const_spec(cre.shape), _const_spec(cim.shape),
                  full((1, d)), _const_spec(w_glu.shape), full((1, 2 * d))],
        out_specs=[full((n, d)), full((n, ns)), full((n, ns))],
        out_shape=[jax.ShapeDtypeStruct((n, d), F32), jax.ShapeDtypeStruct((n, ns), F32),
                   jax.ShapeDtypeStruct((n, ns), F32)],
        scratch_shapes=[pltpu.VMEM((n, d), F32)],
        compiler_params=pltpu.CompilerParams(vmem_limit_bytes=_vmem_limit(need)),
        name="s5_sample",
    )(x, g.reshape(1, d), h_re, h_im, ar, ai, bre, bim, cre, cim, d_skip.reshape(1, d), w_glu,
      b_glu.reshape(1, 2 * d))


def kernel(x_prompt, x_sample, cache_k, cache_v, state_ssm_re, state_ssm_im, page_table, rel_bias, norm_g, final_norm_g, ffn_w_in, ffn_w_out, att_w_qkv, att_w_o, ssm_log_dt, ssm_a_re, ssm_a_im, ssm_b_re, ssm_b_im, ssm_c_re, ssm_c_im, ssm_d, ssm_w_glu, ssm_b_glu):
    bp, t, d = x_prompt.shape
    bs, ts, _ = x_sample.shape
    assert ts == 1
    n_heads = cache_k.shape[3]
    hd = cache_k.shape[4]
    page = cache_k.shape[2]
    n_phys = cache_k.shape[1]
    n_pages = page_table.shape[1]
    past_len = n_pages * page
    assert MOBA_BLOCK % page == 0 and past_len % MOBA_BLOCK == 0
    ppb = MOBA_BLOCK // page
    n_full = past_len // MOBA_BLOCK
    assert n_full >= MOBA_TOPK
    n_groups, n_state = ssm_a_re.shape[1], ssm_a_re.shape[2]
    oct_g = LANES // ssm_b_re.shape[-1]
    n_oct = n_groups // oct_g
    depth = norm_g.shape[0]
    assert depth == 2

    w_in = ffn_w_in.astype(BF16)
    w_out = ffn_w_out.astype(BF16)
    w_qkv = att_w_qkv[0].astype(BF16)
    w_o = att_w_o[0].astype(BF16)
    w_glu = ssm_w_glu[0].astype(BF16)

    xp = x_prompt.reshape(bp * t, d)
    xs = x_sample.reshape(bs, d)

    xp = _ffn(xp, norm_g[0, 0], w_in[0, 0], w_out[0, 0], name="ffn_prompt")
    xs = _ffn(xs, norm_g[0, 0], w_in[0, 0], w_out[0, 0], name="ffn_sample")

    tiles, dec = _bias_tables(rel_bias, n_heads)

    q_s, k_s, v_s = _qkv_sample(xs, norm_g[0, 1], w_qkv)
    ckt = cache_k[0].transpose(0, 2, 3, 1)
    cvt = cache_v[0].transpose(0, 2, 3, 1)

    kt_p, vt_p, qb, kb, vb, km = _qkv_prompt(xp, norm_g[0, 1], w_qkv, float(hd) ** -0.5 * LOG2E, t)
    o_p, idx = _moba_prompt(qb.reshape(bp, t, d), kb.reshape(bp, t, d), vb.reshape(bp, t, d),
                            km.reshape(bp, t // MOBA_BLOCK, d), tiles, rel_bias, page_table, ckt, q_s,
                            n_heads=n_heads, n_full=n_full, ppb=ppb)
    xp = _ffn(xp, norm_g[0, 2], w_in[0, 1], w_out[0, 1], pre=(o_p.reshape(bp * t, d), w_o), name="ffn_prompt_o")

    idx_flat = idx[:, :, :MOBA_TOPK].reshape(-1)
    o_s = _decode_attn(page_table, idx_flat, ckt, cvt, q_s, k_s, v_s, dec, n_full=n_full, ppb=ppb)
    xs = _ffn(xs, norm_g[0, 2], w_in[0, 1], w_out[0, 1], pre=(o_s.reshape(bs, d).astype(BF16), w_o),
              name="ffn_sample_o")

    xp = _ffn(xp, norm_g[1, 0], w_in[1, 0], w_out[1, 0], name="ffn_prompt_l1")
    xs = _ffn(xs, norm_g[1, 0], w_in[1, 0], w_out[1, 0], name="ffn_sample_l1")
    prm = _s5_params(ssm_log_dt[0], ssm_a_re[0], ssm_a_im[0], ssm_b_re[0], ssm_b_im[0], ssm_c_re[0], ssm_c_im[0],
                     oct_g)
    xp3, hr_p, hi_p = _s5_prompt(xp.reshape(bp, t, d), norm_g[1, 1], prm, ssm_d[0], w_glu, ssm_b_glu[0], n_oct=n_oct)
    xs, hr_s, hi_s = _s5_sample(xs, norm_g[1, 1], state_ssm_re[0].reshape(bs, n_groups * n_state),
                                state_ssm_im[0].reshape(bs, n_groups * n_state), prm, ssm_d[0], w_glu,
                                ssm_b_glu[0], n_oct=n_oct)
    xp = _ffn(xp3.reshape(bp * t, d), norm_g[1, 2], w_in[1, 1], w_out[1, 1], final_g=final_norm_g,
              name="ffn_prompt_final")
    xs = _ffn(xs, norm_g[1, 2], w_in[1, 1], w_out[1, 1], final_g=final_norm_g, name="ffn_sample_final")

    st = lambda a, b: a.reshape(1, b, n_groups, n_state)
    tok_major = lambda a: a.reshape(1, bp, n_heads, hd, t).transpose(0, 1, 4, 2, 3)
    return (xp.reshape(bp, t, d), xs.reshape(bs, 1, d), tok_major(kt_p), tok_major(vt_p),
            k_s.reshape(1, bs, 1, n_heads, hd), v_s.reshape(1, bs, 1, n_heads, hd),
            st(hr_p, bp), st(hi_p, bp), st(hr_s, bs), st(hi_s, bs))
```

```python
import functools
import math

import numpy as np
import jax
import jax.numpy as jnp
from jax import lax
from jax.experimental import pallas as pl
from jax.experimental.pallas import tpu as pltpu

F32 = jnp.float32
BF16 = jnp.bfloat16
I32 = jnp.int32

RMS_EPS = 1e-6
NEG = -1e30
LOG2E = math.log2(math.e)
MOBA_BLOCK = 256
MOBA_TOPK = 3
REL_MAX_DIST = 128
LANES = 128
SUBLANES = 8
VMEM_CAP = 60 * 1024 * 1024

_NT = (((1,), (1,)), ((), ()))


def _vmem_limit(nbytes):
    return int(min(VMEM_CAP, nbytes + (8 << 20)))


def _const_spec(shape):
    nd = len(shape)
    return pl.BlockSpec(shape, lambda *_: (0,) * nd, pipeline_mode=pl.Buffered(1))


def _rms(x, g):
    return x * lax.rsqrt(jnp.mean(x * x, axis=-1, keepdims=True) + RMS_EPS) * g


def _split3(x):
    a = x.astype(BF16)
    r = x - a.astype(F32)
    b = r.astype(BF16)
    c = (r - b.astype(F32)).astype(BF16)
    return a, b, c


def _ffn_kernel(*refs, d_ff, chunk, has_pre, has_final):
    it = iter(refs)
    x_ref = next(it)
    if has_pre:
        a_ref, wpre_ref = next(it), next(it)
    g_ref, win_ref, wout_ref = next(it), next(it), next(it)
    if has_final:
        gf_ref = next(it)
    o_ref, act_ref = next(it), next(it)

    x = x_ref[...]
    if has_pre:
        x = x + jnp.dot(a_ref[...], wpre_ref[...], preferred_element_type=F32)
    xn = _rms(x, g_ref[...]).astype(BF16)
    for c in range(d_ff // chunk):
        lo = c * chunk
        gate = jnp.dot(xn, win_ref[:, lo:lo + chunk], preferred_element_type=F32)
        up = jnp.dot(xn, win_ref[:, d_ff + lo:d_ff + lo + chunk], preferred_element_type=F32)
        act_ref[:, lo:lo + chunk] = (gate * jax.nn.sigmoid(gate) * up).astype(BF16)
    y = x + 0.5 * jnp.dot(act_ref[...], wout_ref[...], preferred_element_type=F32)
    if has_final:
        y = _rms(y, gf_ref[...])
    o_ref[...] = y


def _ffn(x, g, w_in, w_out, wsel, *, pre=None, final_g=None, name):
    n, d = x.shape
    d_ff = w_out.shape[2]
    tm = min(n, 512)
    chunk = 256
    assert n % tm == 0 and d_ff % chunk == 0
    row = lambda i: (i, 0)
    picked = lambda w: pl.BlockSpec((None, None) + w.shape[2:], lambda *_: wsel + (0, 0),
                                    pipeline_mode=pl.Buffered(1))
    args, specs = [x], [pl.BlockSpec((tm, d), row)]
    if pre is not None:
        a, w_pre = pre
        args += [a, w_pre]
        specs += [pl.BlockSpec((tm, a.shape[1]), row), _const_spec(w_pre.shape)]
    args += [g.reshape(1, d), w_in, w_out]
    specs += [_const_spec((1, d)), picked(w_in), picked(w_out)]
    if final_g is not None:
        args.append(final_g.reshape(1, d))
        specs.append(_const_spec((1, d)))
    wbytes = 2 * (d * 2 * d_ff + d_ff * d + (pre[1].size if pre is not None else 0))
    need = wbytes + tm * d * 4 * 4 + tm * d_ff * 2 + tm * chunk * 4 * 6 + tm * d * 4 * 3
    return pl.pallas_call(
        functools.partial(_ffn_kernel, d_ff=d_ff, chunk=chunk, has_pre=pre is not None,
                          has_final=final_g is not None),
        grid=(n // tm,),
        in_specs=specs,
        out_specs=pl.BlockSpec((tm, d), row),
        out_shape=jax.ShapeDtypeStruct((n, d), F32),
        scratch_shapes=[pltpu.VMEM((tm, d_ff), BF16)],
        compiler_params=pltpu.CompilerParams(dimension_semantics=("parallel",),
                                             vmem_limit_bytes=_vmem_limit(need)),
        name=name,
    )(*args)


def _qkv_prompt_kernel(x_ref, g_ref, w_ref, kt_ref, vt_ref, qb_ref, kb_ref, vb_ref, km_ref, *, d, scale):
    xn = _rms(x_ref[...], g_ref[...]).astype(BF16)
    q = jnp.dot(xn, w_ref[:, 0:d], preferred_element_type=F32)
    qb_ref[...] = (q * scale).astype(BF16)
    k = jnp.dot(xn, w_ref[:, d:2 * d], preferred_element_type=F32)
    kt_ref[0] = k.T
    kb_ref[...] = k.astype(BF16)
    for r in range(k.shape[0] // MOBA_BLOCK):
        km_ref[r] = jnp.sum(k[r * MOBA_BLOCK:(r + 1) * MOBA_BLOCK], axis=0, keepdims=True) * (1.0 / MOBA_BLOCK)
    v = jnp.dot(xn, w_ref[:, 2 * d:3 * d], preferred_element_type=F32)
    vt_ref[0] = v.T
    vb_ref[...] = v.astype(BF16)


def _qkv_prompt(x, g, w_qkv, scale, t):
    n, d = x.shape
    tm = 512
    assert n % t == 0 and t % tm == 0 and tm % MOBA_BLOCK == 0
    tpb = t // tm
    row = lambda i: (i, 0)
    rs = pl.BlockSpec((tm, d), row)
    ts = pl.BlockSpec((1, d, tm), lambda i: (i // tpb, 0, i % tpb))
    need = 2 * w_qkv.size + tm * d * (4 * 2 * 3 + 2 * 2 * 3) + tm * d * 4 * 6
    return pl.pallas_call(
        functools.partial(_qkv_prompt_kernel, d=d, scale=scale),
        grid=(n // tm,),
        in_specs=[rs, _const_spec((1, d)), _const_spec(w_qkv.shape)],
        out_specs=[ts, ts, rs, rs, rs,
                   pl.BlockSpec((tm // MOBA_BLOCK, 1, d), lambda i: (i, 0, 0))],
        out_shape=[jax.ShapeDtypeStruct((n // t, d, t), F32), jax.ShapeDtypeStruct((n // t, d, t), F32),
                   jax.ShapeDtypeStruct((n, d), BF16), jax.ShapeDtypeStruct((n, d), BF16),
                   jax.ShapeDtypeStruct((n, d), BF16),
                   jax.ShapeDtypeStruct((n // MOBA_BLOCK, 1, d), F32)],
        compiler_params=pltpu.CompilerParams(dimension_semantics=("parallel",),
                                             vmem_limit_bytes=_vmem_limit(need)),
        name="qkv_prompt",
    )(x, g.reshape(1, d), w_qkv)


def _qkv_sample_kernel(x_ref, g_ref, w_ref, q_ref, k_ref, v_ref, *, d):
    xn = _rms(x_ref[...], g_ref[...]).astype(BF16)
    q_ref[...] = jnp.dot(xn, w_ref[:, 0:d], preferred_element_type=F32)
    k_ref[...] = jnp.dot(xn, w_ref[:, d:2 * d], preferred_element_type=F32)
    v_ref[...] = jnp.dot(xn, w_ref[:, 2 * d:3 * d], preferred_element_type=F32)


def _qkv_sample(x, g, w_qkv):
    n, d = x.shape
    full = pl.BlockSpec((n, d), lambda i: (0, 0))
    sd = jax.ShapeDtypeStruct((n, d), F32)
    return pl.pallas_call(
        functools.partial(_qkv_sample_kernel, d=d),
        grid=(1,),
        in_specs=[full, _const_spec((1, d)), _const_spec(w_qkv.shape)],
        out_specs=[full, full, full],
        out_shape=[sd, sd, sd],
        compiler_params=pltpu.CompilerParams(vmem_limit_bytes=_vmem_limit(2 * w_qkv.size + 16 * n * d)),
        name="qkv_sample",
    )(x, g.reshape(1, d), w_qkv)


def _rel_bucket_np(n, n_buckets):
    n = np.maximum(n, 0)
    max_exact = n_buckets // 2
    nf = np.maximum(n, max_exact).astype(np.float32)
    large = max_exact + (np.log(nf / np.float32(max_exact)) / np.float32(math.log(REL_MAX_DIST / max_exact))
                         * np.float32(n_buckets - max_exact)).astype(np.int32)
    large = np.minimum(large, n_buckets - 1)
    return np.where(n < max_exact, n, large).astype(np.int32)


def _bias_kernel(rel_ref, bkt_ref, dbk_ref, tile_ref, dec_ref, *, n_buckets):
    h = pl.program_id(0)
    for t in range(2):
        b = bkt_ref[t]
        acc = jnp.full(b.shape, NEG, F32)
        for u in range(n_buckets):
            acc = jnp.where(b == u, rel_ref[u, h], acc)
        tile_ref[0, t] = acc * LOG2E
    b = dbk_ref[...]
    acc = jnp.zeros(b.shape, F32)
    for u in range(n_buckets):
        acc = jnp.where(b == u, rel_ref[u, h], acc)
    dec_ref[0] = acc


def _bias_tables(rel_bias, n_heads):
    n_buckets = rel_bias.shape[0]
    blk = MOBA_BLOCK
    qi = np.arange(blk)[:, None]
    ki = np.arange(blk)[None, :]
    own = np.where(ki <= qi, _rel_bucket_np(qi - ki, n_buckets), -1)
    adj = _rel_bucket_np(blk + qi - ki, n_buckets)
    bkt = np.stack([own, adj]).astype(np.int32)
    dbk = np.zeros((SUBLANES, blk), np.int32)
    dbk[0] = _rel_bucket_np(blk - np.arange(blk), n_buckets)
    dbk[1] = _rel_bucket_np(np.full((blk,), 2 * blk), n_buckets)
    dbk[2] = 0
    return pl.pallas_call(
        functools.partial(_bias_kernel, n_buckets=n_buckets),
        grid=(n_heads,),
        in_specs=[pl.BlockSpec(memory_space=pltpu.SMEM),
                  pl.BlockSpec((2, blk, blk), lambda h: (0, 0, 0)),
                  pl.BlockSpec((SUBLANES, blk), lambda h: (0, 0))],
        out_specs=[pl.BlockSpec((1, 2, blk, blk), lambda h: (h, 0, 0, 0)),
                   pl.BlockSpec((1, SUBLANES, blk), lambda h: (h, 0, 0))],
        out_shape=[jax.ShapeDtypeStruct((n_heads, 2, blk, blk), F32),
                   jax.ShapeDtypeStruct((n_heads, SUBLANES, blk), F32)],
        name="rel_bias_tables",
    )(rel_bias, jnp.asarray(bkt), jnp.asarray(dbk))


def _moba_kernel(pt_ref, rel_ref, q_ref, k_ref, v_ref, km_ref, tile_ref, *rest, n_blk, hd, far_bucket, gate):
    n_gate_pages = gate["bps"] * gate["ppb"]
    gate_pages = rest[0:n_gate_pages]
    qs_ref, o_ref, idx_ref, q2_ref, s_ref, m_ref, acc_ref, qb_ref, g_ref = rest[n_gate_pages:]
    blk = MOBA_BLOCK
    half = 2 * blk
    rows = 2 * half
    dead = 2 * n_blk
    hp = pl.program_id(1)
    p = pl.program_id(2)
    lane = lax.broadcasted_iota(I32, (blk, LANES), 1)

    step = (pl.program_id(0) * pl.num_programs(1) + hp) * pl.num_programs(2) + p
    chunk = step % gate["chunks"]
    _decode_gate_step(gate_pages, qs_ref, qb_ref, g_ref, chunk, ppb=gate["ppb"], bps=gate["bps"],
                      n_heads=gate["n_heads"], hd=gate["hd"])

    def two_heads(q):
        return [jnp.where(lane < hd, q, jnp.zeros_like(q)), jnp.where(lane >= hd, q, jnp.zeros_like(q))]

    q2 = jnp.concatenate(two_heads(q_ref[0, 0:blk, :]) + two_heads(q_ref[0, blk:2 * blk, :]), axis=0)
    km = km_ref[0]
    km_hi = km.astype(BF16)
    km_lo = (km - km_hi.astype(F32)).astype(BF16)
    gt = (lax.dot_general(km_hi, q2, _NT, preferred_element_type=F32)
          + lax.dot_general(km_lo, q2, _NT, preferred_element_type=F32))
    bidx = lax.broadcasted_iota(I32, (n_blk, rows), 0)
    qcol = lax.broadcasted_iota(I32, (n_blk, rows), 1)
    i_col = 2 * p + jnp.where(qcol >= half, 1, 0)
    valid = bidx < i_col
    gm = jnp.where(valid, gt, NEG)
    rank = jnp.zeros((n_blk, rows), I32)
    for jp in range(n_blk):
        row = gm[jp:jp + 1, :]
        rank = rank + jnp.where(row > gm, 1, jnp.where(row == gm, jnp.where(bidx > jp, 1, 0), 0))
    far = jnp.where((qcol & blk) == 0, rel_ref[far_bucket, 2 * hp], rel_ref[far_bucket, 2 * hp + 1]) * LOG2E
    chosen = jnp.where(rank < MOBA_TOPK, jnp.where(bidx <= i_col - 2, far, 0.0), NEG)
    aval = jnp.where(valid, chosen, jnp.where(bidx == i_col, 0.0, NEG))
    a_hi = aval.astype(BF16).astype(F32)
    a_lo = (aval - a_hi).astype(BF16).astype(F32)
    dead_rows = jnp.where(lax.broadcasted_iota(I32, (SUBLANES, rows), 0) == 0, NEG, 0.0)
    at = jnp.concatenate([a_hi, a_lo, dead_rows, jnp.zeros((LANES - dead - SUBLANES, rows), F32)], axis=0)
    q2_ref[:, 0:LANES] = q2
    q2_ref[:, LANES:2 * LANES] = at.T.astype(BF16)

    ones_col = jnp.where(lane == 0, 1.0, 0.0).astype(BF16)
    n_far = jnp.maximum(2 * p - 1, 0)

    def scores(j, masked, r0=0):
        kj = k_ref[0, pl.ds(pl.multiple_of(j * blk, blk), blk), :]
        c_hi = jnp.where(masked, dead, j)
        c_lo = jnp.where(masked, dead, n_blk + j)
        ej = jnp.where((lane == c_hi) | (lane == c_lo), 1.0, 0.0).astype(BF16)
        return lax.dot_general(q2_ref[r0:rows], jnp.concatenate([kj, ej], axis=1), _NT, preferred_element_type=F32)

    def v_aug(j):
        vj = v_ref[0, pl.ds(pl.multiple_of(j * blk, blk), blk), :]
        return jnp.concatenate([vj, ones_col], axis=1)

    def lane_max(s):
        return jnp.maximum(s[:, 0:LANES], s[:, LANES:2 * LANES])

    def far_id(j):
        pad = j >= n_far
        return jnp.where(pad, n_blk - 1, j), pad

    t_own = jnp.concatenate([tile_ref[0, 0], tile_ref[1, 0]], axis=0)
    t_adj = jnp.concatenate([tile_ref[0, 1], tile_ref[1, 1]], axis=0)
    s0 = scores(jnp.maximum(2 * p - 1, 0), p == 0)
    s1 = scores(2 * p, False)
    s0a = s0[0:half] + t_adj
    s1a = s1[0:half] + t_own
    s1b = s1[half:rows] + t_adj
    s2b = scores(2 * p + 1, False, half) + t_own
    s_ref[0, 0:half] = s0a
    s_ref[0, half:rows] = s0[half:rows]
    s_ref[1, 0:half] = s1a
    s_ref[1, half:rows] = s1b
    s_ref[2, half:rows] = s2b
    m_ref[0:half] = jnp.maximum(lane_max(s0a), lane_max(s1a))
    m_ref[half:rows] = jnp.maximum(jnp.maximum(lane_max(s0[half:rows]), lane_max(s1b)), lane_max(s2b))

    n4 = (n_far + 1) // 4
    tail2 = n_far - 4 * n4 == 1

    def a_group(base, width):
        mm = m_ref[...]
        for u in range(width):
            j = base + u
            s = scores(*far_id(j))
            s_ref[3 + j] = s
            mm = jnp.maximum(mm, lane_max(s))
        m_ref[...] = mm

    def a_trip(it, c):
        a_group(it * 4, 4)
        return c

    lax.fori_loop(0, n4, a_trip, 0)

    @pl.when(tail2)
    def _():
        a_group(4 * n4, 2)

    m_ref[...] = jnp.broadcast_to(jnp.max(m_ref[...], axis=-1, keepdims=True), m_ref.shape)

    def prob(slot, r0=0):
        mb = m_ref[r0:rows]
        return jnp.exp2(s_ref[slot, r0:rows] - jnp.concatenate([mb, mb], axis=1)).astype(BF16)

    acc_ref[...] = (jnp.dot(prob(0), v_aug(jnp.maximum(2 * p - 1, 0)), preferred_element_type=F32)
                    + jnp.dot(prob(1), v_aug(2 * p), preferred_element_type=F32))
    acc_ref[half:rows] += jnp.dot(prob(2, half), v_aug(2 * p + 1), preferred_element_type=F32)

    def b_group(base, width):
        part = None
        for u in range(width):
            j = base + u
            d_ = jnp.dot(prob(3 + j), v_aug(far_id(j)[0]), preferred_element_type=F32)
            part = d_ if part is None else part + d_
        acc_ref[...] += part

    def b_trip(it, c):
        b_group(it * 4, 4)
        return c

    lax.fori_loop(0, n4, b_trip, 0)

    @pl.when(tail2)
    def _():
        b_group(4 * n4, 2)

    acc = acc_ref[...]
    for qb in range(2):
        r0 = qb * half
        o0 = acc[r0:r0 + blk, 0:LANES] / acc[r0:r0 + blk, LANES:LANES + 1]
        o1 = acc[r0 + blk:r0 + half, 0:LANES] / acc[r0 + blk:r0 + half, LANES:LANES + 1]
        o_ref[0, qb * blk:(qb + 1) * blk, :] = jnp.where(lane < hd, o0, o1).astype(BF16)

    _decode_gate_topk(idx_ref, g_ref, chunk, gate["chunks"], n_full=gate["n_full"])


def _moba_prompt(qb, kb, vb, km, tiles, rel_bias, page_table, cache_kt, q_s, *, n_heads, n_full, ppb):
    b, t, d = qb.shape
    hd = d // n_heads
    assert 2 * hd == LANES and t % MOBA_BLOCK == 0
    n_blk = t // MOBA_BLOCK
    assert 2 * n_blk + SUBLANES <= LANES and n_blk % 2 == 0
    blk = MOBA_BLOCK
    rows = 4 * blk
    far_bucket = int(_rel_bucket_np(np.array([2 * blk]), rel_bias.shape[0])[0])
    assert int(_rel_bucket_np(np.array([blk + 1]), rel_bias.shape[0])[0]) == far_bucket
    n_hp, n_p = n_heads // 2, n_blk // 2

    ns, ds = q_s.shape
    n_phys, heads_s, hd_s, page = cache_kt.shape
    assert page == LANES and n_full <= LANES
    n_pages = page_table.shape[1]
    n_steps = b * n_hp * n_p
    assert (ns * n_full) % n_steps == 0
    bps = ns * n_full // n_steps
    assert n_full % bps == 0
    chunks = n_full // bps
    gate = dict(n_full=n_full, ppb=ppb, bps=bps, n_heads=heads_s, hd=hd_s, chunks=chunks)

    def step_of(bi, hp, p):
        return (bi * n_hp + hp) * n_p + p

    def page_spec(bb, r):
        def imap(bi, hp, p, pt):
            st = step_of(bi, hp, p)
            return (pt[(st // chunks) * n_pages + ((st % chunks) * bps + bb) * ppb + r], 0, 0, 0)
        return pl.BlockSpec((1, heads_s, hd_s, page), imap)

    seq = pl.BlockSpec((1, t, LANES), lambda bi, hp, p, pt: (bi, 0, hp))
    qblk = pl.BlockSpec((1, 2 * blk, LANES), lambda bi, hp, p, pt: (bi, p, hp))
    per_seq = lambda width: pl.BlockSpec((1, width[0], width[1]),
                                         lambda bi, hp, p, pt: (step_of(bi, hp, p) // chunks, 0, 0))
    n_slots = 3 + n_blk
    need = (2 * 2 * t * LANES * 2 + n_slots * rows * blk * 4 + 2 * 4 * blk * blk * 4
            + rows * (2 * LANES * 2 + LANES * 4 + 2 * LANES * 4)
            + 2 * bps * ppb * ds * page * 4 + ds * LANES * 4 + (8 << 20))
    grid_spec = pltpu.PrefetchScalarGridSpec(
        num_scalar_prefetch=1,
        grid=(b, n_hp, n_p),
        in_specs=[pl.BlockSpec(memory_space=pltpu.SMEM), qblk, seq, seq,
                  pl.BlockSpec((1, n_blk, LANES), lambda bi, hp, p, pt: (bi, 0, hp)),
                  pl.BlockSpec((2, 2, blk, blk), lambda bi, hp, p, pt: (hp, 0, 0, 0))]
                 + [page_spec(bb, r) for bb in range(bps) for r in range(ppb)]
                 + [per_seq((1, ds))],
        out_specs=[qblk, per_seq((heads_s, LANES))],
        scratch_shapes=[pltpu.VMEM((rows, 2 * LANES), BF16), pltpu.VMEM((n_slots, rows, blk), F32),
                        pltpu.VMEM((rows, LANES), F32), pltpu.VMEM((rows, 2 * LANES), F32),
                        pltpu.VMEM((ds, LANES), F32), pltpu.VMEM((heads_s, LANES), F32)],
    )
    return pl.pallas_call(
        functools.partial(_moba_kernel, n_blk=n_blk, hd=hd, far_bucket=far_bucket, gate=gate),
        grid_spec=grid_spec,
        out_shape=[jax.ShapeDtypeStruct((b, t, d), BF16), jax.ShapeDtypeStruct((ns, heads_s, LANES), I32)],
        compiler_params=pltpu.CompilerParams(dimension_semantics=("arbitrary", "arbitrary", "arbitrary"),
                                             vmem_limit_bytes=_vmem_limit(need)),
        name="moba_prompt",
    )(page_table.reshape(-1), rel_bias, qb, kb, vb, km, tiles, *([cache_kt] * (bps * ppb)),
      q_s.reshape(ns, 1, ds))


def _decode_gate_step(pages, q_ref, qb_ref, g_ref, jj, *, ppb, bps, n_heads, hd):
    lane = lax.broadcasted_iota(I32, (n_heads, LANES), 1)

    @pl.when(jj == 0)
    def _():
        q128 = jnp.broadcast_to(q_ref[0], (LANES, n_heads * hd))
        for c in range(n_heads * hd // LANES):
            qb_ref[c * LANES:(c + 1) * LANES, :] = q128[:, c * LANES:(c + 1) * LANES].T
        g_ref[...] = jnp.zeros_like(g_ref)

    head = lax.broadcasted_iota(I32, (n_heads, LANES), 0)
    g = g_ref[...]
    for h in range(n_heads):
        accs = [None] * bps
        for c in range(hd // SUBLANES):
            rows = slice(c * SUBLANES, (c + 1) * SUBLANES)
            qv = qb_ref[h * hd + c * SUBLANES:h * hd + (c + 1) * SUBLANES, :]
            for bb in range(bps):
                kk = pages[bb * ppb][0, h, rows, :]
                for r in range(1, ppb):
                    kk = kk + pages[bb * ppb + r][0, h, rows, :]
                accs[bb] = kk * qv if accs[bb] is None else accs[bb] + kk * qv
        for bb in range(bps):
            tot = jnp.sum(jnp.sum(accs[bb], axis=0, keepdims=True), axis=-1, keepdims=True) * (1.0 / MOBA_BLOCK)
            g = jnp.where(lane == jj * bps + bb, jnp.where(head == h, tot, g), g)
    g_ref[...] = g


def _decode_gate_topk(idx_ref, g_ref, jj, n_chunks, *, n_full):
    n_heads = g_ref.shape[0]
    lane = lax.broadcasted_iota(I32, (n_heads, LANES), 1)

    @pl.when(jj == n_chunks - 1)
    def _():
        g = jnp.where(lane < n_full, g_ref[...], -3e38)
        out = jnp.zeros((n_heads, LANES), I32)
        for s in range(MOBA_TOPK):
            mx = jnp.max(g, axis=-1, keepdims=True)
            ix = jnp.min(jnp.where(g == mx, lane, LANES), axis=-1, keepdims=True)
            out = jnp.where(lane == s, ix, out)
            g = jnp.where(lane == ix, -3e38, g)
        idx_ref[0] = out


def _decode_attn_kernel(pt_ref, idx_ref, *refs, n_slots, ppb, n_full, hps, hd, scale):
    n_pg = hps * n_slots * ppb
    kp = refs[0:n_pg]
    vp = refs[n_pg:2 * n_pg]
    q_ref, kn_ref, vn_ref, dec_ref, o_ref = refs[2 * n_pg:]
    bg = pl.program_id(0) * pl.num_programs(1) + pl.program_id(1)
    page = kp[0].shape[-1]
    outs = []
    for hh in range(hps):
        cols = slice(hh * hd, (hh + 1) * hd)
        q = q_ref[0][:, cols]
        q8 = jnp.broadcast_to(q, (SUBLANES, hd)).astype(BF16)
        dec = dec_ref[hh]
        s_list, v_list = [], []
        for s in range(n_slots):
            blk_id = idx_ref[(bg * hps + hh) * n_slots + s]
            bias = jnp.where(blk_id == n_full - 1, dec[0:1, :], dec[1:2, :])
            for r in range(ppb):
                pg = (hh * n_slots + s) * ppb + r
                kt = kp[pg][0, 0].astype(BF16)
                sc = jnp.dot(q8, kt, preferred_element_type=F32) * scale + bias[:, r * page:(r + 1) * page]
                s_list.append(sc)
                v_list.append(vp[pg][0, 0].astype(BF16))
        s_self = jnp.sum(q * kn_ref[0][:, cols], axis=-1, keepdims=True) * scale + dec[2:3, 0:1]
        m = s_self
        for sc in s_list:
            m = jnp.maximum(m, jnp.max(sc, axis=-1, keepdims=True))
        p_self = jnp.exp(s_self - m)
        l = p_self
        acc = p_self * vn_ref[0][:, cols]
        for sc, vt in zip(s_list, v_list):
            p = jnp.exp(sc - m)
            l = l + jnp.sum(p, axis=-1, keepdims=True)
            acc = acc + lax.dot_general(p.astype(BF16), vt, _NT, preferred_element_type=F32)
        outs.append((acc / l)[0:1, :])
    o_ref[0] = jnp.concatenate(outs, axis=1)


def _decode_attn(page_table, idx_flat, cache_kt, cache_vt, q, k_new, v_new, dec, *, n_full, ppb):
    nb, d = q.shape
    n_phys, n_heads, hd, page = cache_kt.shape
    n_pages = page_table.shape[1]
    n_slots = MOBA_TOPK
    hps = 4
    assert n_heads % hps == 0
    n_grp = n_heads // hps

    def page_spec(hh, s, r):
        def imap(b, g, pt, idx):
            h = g * hps + hh
            blk_id = idx[(b * n_heads + h) * n_slots + s]
            return (pt[b * n_pages + blk_id * ppb + r], h, 0, 0)
        return pl.BlockSpec((1, 1, hd, page), imap)

    grp = pl.BlockSpec((1, 1, hps * hd), lambda b, g, pt, idx: (b * n_grp + g, 0, 0))
    pspecs = [page_spec(hh, s, r) for hh in range(hps) for s in range(n_slots) for r in range(ppb)]
    grid_spec = pltpu.PrefetchScalarGridSpec(
        num_scalar_prefetch=2,
        grid=(nb, n_grp),
        in_specs=pspecs + pspecs + [grp, grp, grp,
                                    pl.BlockSpec((hps, SUBLANES, MOBA_BLOCK), lambda b, g, pt, idx: (g, 0, 0))],
        out_specs=grp,
    )
    n_pg = hps * n_slots * ppb
    per_grp = lambda a: a.reshape(nb * n_grp, 1, hps * hd)
    return pl.pallas_call(
        functools.partial(_decode_attn_kernel, n_slots=n_slots, ppb=ppb, n_full=n_full, hps=hps, hd=hd,
                          scale=float(hd) ** -0.5),
        grid_spec=grid_spec,
        out_shape=jax.ShapeDtypeStruct((nb * n_grp, 1, hps * hd), F32),
        compiler_params=pltpu.CompilerParams(dimension_semantics=("arbitrary", "arbitrary")),
        name="decode_attn",
    )(page_table.reshape(-1), idx_flat, *([cache_kt] * n_pg), *([cache_vt] * n_pg),
      per_grp(q), per_grp(k_new), per_grp(v_new), dec)


def _s5_params(log_dt, a_re, a_im, b_re, b_im, c_re, c_im, oct_g):
    g, p = a_re.shape
    gd = b_re.shape[-1]
    dt = jnp.exp(log_dt.astype(F32))[:, None]
    ar, ai = a_re.astype(F32), a_im.astype(F32)
    mag = jnp.exp(dt * ar)
    abar_re, abar_im = mag * jnp.cos(dt * ai), mag * jnp.sin(dt * ai)
    den = ar * ar + ai * ai
    coef_re = ((abar_re - 1.0) * ar + abar_im * ai) / den
    coef_im = (abar_im * ar - (abar_re - 1.0) * ai) / den
    br, bim = b_re.astype(F32), b_im.astype(F32)
    bbar_re = coef_re[..., None] * br - coef_im[..., None] * bim
    bbar_im = coef_re[..., None] * bim + coef_im[..., None] * br
    n_oct = g // oct_g
    eye = jnp.eye(oct_g, dtype=F32)

    def in_blocks(bb):
        bb = bb.reshape(n_oct, oct_g, p, gd).transpose(0, 1, 3, 2)
        return jnp.einsum('ogip,gh->ogihp', bb, eye).reshape(n_oct, oct_g * gd, oct_g * p)

    def out_blocks(cc):
        cc = cc.reshape(n_oct, oct_g, gd, p).transpose(0, 1, 3, 2)
        return jnp.einsum('ogpi,gh->ogphi', cc, eye).reshape(n_oct, oct_g * p, oct_g * gd)

    return (abar_re.reshape(1, g * p), abar_im.reshape(1, g * p), in_blocks(bbar_re), in_blocks(bbar_im),
            out_blocks(c_re.astype(F32)), out_blocks(c_im.astype(F32)))


def _glu_tail(y, xn, d_ref, wg_ref, bg_ref, d):
    z = jax.nn.gelu(y + d_ref[...] * xn).astype(BF16)
    gl = jnp.dot(z, wg_ref[...], preferred_element_type=F32) + bg_ref[...]
    return gl[:, 0:d] * jax.nn.sigmoid(gl[:, d:2 * d])


def _s5_prompt_kernel(x_ref, g_ref, ar_ref, ai_ref, bre_ref, bim_ref, cre_ref, cim_ref, d_ref, wg_ref, bg_ref,
                      o_ref, hr_ref, hi_ref, xs_ref, un_ref, ub_ref, sre_ref, sim_ref, y_ref, *, n_oct, lc):
    nb, _, d = x_ref.shape
    rows = nb * lc
    ow = d // n_oct
    sw = ar_ref.shape[1] // n_oct

    @pl.when(pl.program_id(0) == 0)
    def _():
        hr_ref[...] = jnp.zeros_like(hr_ref)
        hi_ref[...] = jnp.zeros_like(hi_ref)

    xs_ref[...] = _rms(x_ref[...], g_ref[...])

    def to_time_major(t, c):
        un_ref[pl.ds(pl.multiple_of(t * nb, nb), nb), :] = xs_ref[:, pl.ds(t, 1), :].reshape(nb, d)
        return c

    lax.fori_loop(0, lc, to_time_major, 0, unroll=True)
    ub_ref[...] = un_ref[...].astype(BF16)

    for o in range(n_oct):
        uo = ub_ref[:, o * ow:(o + 1) * ow]
        sre_ref[...] = jnp.dot(uo, bre_ref[o], preferred_element_type=F32)
        sim_ref[...] = jnp.dot(uo, bim_ref[o], preferred_element_type=F32)
        ar = ar_ref[:, o * sw:(o + 1) * sw]
        ai = ai_ref[:, o * sw:(o + 1) * sw]

        def step(t, carry):
            hr, hi = carry
            r = pl.ds(pl.multiple_of(t * nb, nb), nb)
            nr = ar * hr - ai * hi + sre_ref[r, :]
            ni = ar * hi + ai * hr + sim_ref[r, :]
            sre_ref[r, :] = nr
            sim_ref[r, :] = ni
            return nr, ni

        hr, hi = lax.fori_loop(0, lc, step, (hr_ref[:, o * sw:(o + 1) * sw], hi_ref[:, o * sw:(o + 1) * sw]),
                               unroll=True)
        hr_ref[:, o * sw:(o + 1) * sw] = hr
        hi_ref[:, o * sw:(o + 1) * sw] = hi
        y_ref[:, o * ow:(o + 1) * ow] = (
            jnp.dot(sre_ref[...].astype(BF16), cre_ref[o], preferred_element_type=F32)
            - jnp.dot(sim_ref[...].astype(BF16), cim_ref[o], preferred_element_type=F32))

    y_ref[...] = _glu_tail(y_ref[...], un_ref[...], d_ref, wg_ref, bg_ref, d)

    def to_batch_major(t, c):
        mix = y_ref[pl.ds(pl.multiple_of(t * nb, nb), nb), :].reshape(nb, 1, d)
        o_ref[:, pl.ds(t, 1), :] = x_ref[:, pl.ds(t, 1), :] + mix
        return c

    lax.fori_loop(0, lc, to_batch_major, 0, unroll=True)


def _s5_prompt(x, g, prm, d_skip, w_glu, b_glu, *, n_oct):
    nb, t, d = x.shape
    assert nb == SUBLANES
    ar, ai, bre, bim, cre, cim = prm
    ns = ar.shape[1]
    lc = 64
    assert t % lc == 0
    rows = nb * lc
    ar8 = jnp.broadcast_to(ar, (nb, ns))
    ai8 = jnp.broadcast_to(ai, (nb, ns))
    xblk = pl.BlockSpec((nb, lc, d), lambda c: (0, c, 0))
    st = pl.BlockSpec((nb, ns), lambda c: (0, 0))
    need = (4 * nb * lc * d * 4 + 3 * rows * d * 4 + rows * d * 2 + 2 * rows * (ns // n_oct) * 4
            + 2 * (bre.size * 2 + cre.size * 2 + w_glu.size) + rows * 2 * d * 4 + 4 * nb * ns * 4)
    return pl.pallas_call(
        functools.partial(_s5_prompt_kernel, n_oct=n_oct, lc=lc),
        grid=(t // lc,),
        in_specs=[xblk, _const_spec((1, d)), _const_spec((nb, ns)), _const_spec((nb, ns)),
                  _const_spec(bre.shape), _const_spec(bim.shape), _const_spec(cre.shape), _const_spec(cim.shape),
                  _const_spec((1, d)), _const_spec(w_glu.shape), _const_spec((1, 2 * d))],
        out_specs=[xblk, st, st],
        out_shape=[jax.ShapeDtypeStruct((nb, t, d), F32), jax.ShapeDtypeStruct((nb, ns), F32),
                   jax.ShapeDtypeStruct((nb, ns), F32)],
        scratch_shapes=[pltpu.VMEM((nb, lc, d), F32), pltpu.VMEM((rows, d), F32), pltpu.VMEM((rows, d), BF16),
                        pltpu.VMEM((rows, ns // n_oct), F32), pltpu.VMEM((rows, ns // n_oct), F32),
                        pltpu.VMEM((rows, d), F32)],
        compiler_params=pltpu.CompilerParams(dimension_semantics=("arbitrary",),
                                             vmem_limit_bytes=_vmem_limit(need)),
        name="s5_prompt",
    )(x, g.reshape(1, d), ar8, ai8, bre.astype(BF16), bim.astype(BF16), cre.astype(BF16), cim.astype(BF16),
      d_skip.reshape(1, d), w_glu, b_glu.reshape(1, 2 * d))


def _s5_sample_kernel(x_ref, g_ref, h0r_ref, h0i_ref, ar_ref, ai_ref, bre_ref, bim_ref, cre_ref, cim_ref,
                      d_ref, wg_ref, bg_ref, o_ref, hr_ref, hi_ref, y_ref, *, n_oct):
    d = x_ref.shape[1]
    ow = d // n_oct
    sw = ar_ref.shape[1] // n_oct
    x = x_ref[...]
    xn = _rms(x, g_ref[...])
    u = _split3(xn)

    def dot3(lhs3, w):
        w1, w2, w3 = _split3(w)
        l1, l2, l3 = lhs3
        dd = lambda p, q_: jnp.dot(p, q_, preferred_element_type=F32)
        return ((dd(l1, w1) + dd(l1, w2)) + (dd(l2, w1) + dd(l1, w3))) + (dd(l2, w2) + dd(l3, w1))

    for o in range(n_oct):
        uo = tuple(p[:, o * ow:(o + 1) * ow] for p in u)
        ar = ar_ref[:, o * sw:(o + 1) * sw]
        ai = ai_ref[:, o * sw:(o + 1) * sw]
        hr = h0r_ref[:, o * sw:(o + 1) * sw]
        hi = h0i_ref[:, o * sw:(o + 1) * sw]
        nr = ar * hr - ai * hi + dot3(uo, bre_ref[o])
        ni = ar * hi + ai * hr + dot3(uo, bim_ref[o])
        hr_ref[:, o * sw:(o + 1) * sw] = nr
        hi_ref[:, o * sw:(o + 1) * sw] = ni
        y_ref[:, o * ow:(o + 1) * ow] = (
            jnp.dot(nr.astype(BF16), cre_ref[o].astype(BF16), preferred_element_type=F32)
            - jnp.dot(ni.astype(BF16), cim_ref[o].astype(BF16), preferred_element_type=F32))
    o_ref[...] = x + _glu_tail(y_ref[...], xn, d_ref, wg_ref, bg_ref, d)


def _s5_sample(x, g, h_re, h_im, prm, d_skip, w_glu, b_glu, *, n_oct):
    n, d = x.shape
    ar, ai, bre, bim, cre, cim = prm
    ns = ar.shape[1]
    full = lambda shape: pl.BlockSpec(shape, lambda i: (0,) * len(shape))
    need = 4 * (bre.size * 2 + cre.size * 2) + 2 * w_glu.size + 16 * n * ns * 4
    return pl.pallas_call(
        functools.partial(_s5_sample_kernel, n_oct=n_oct),
        grid=(1,),
        in_specs=[full((n, d)), full((1, d)), full((n, ns)), full((n, ns)), full((1, ns)), full((1, ns)),
                  _const_spec(bre.shape), _const_spec(bim.shape), _const_spec(cre.shape), _const_spec(cim.shape),
                  full((1, d)), _const_spec(w_glu.shape), full((1, 2 * d))],
        out_specs=[full((n, d)), full((n, ns)), full((n, ns))],
        out_shape=[jax.ShapeDtypeStruct((n, d), F32), jax.ShapeDtypeStruct((n, ns), F32),
                   jax.ShapeDtypeStruct((n, ns), F32)],
        scratch_shapes=[pltpu.VMEM((n, d), F32)],
        compiler_params=pltpu.CompilerParams(vmem_limit_bytes=_vmem_limit(need)),
        name="s5_sample",
    )(x, g.reshape(1, d), h_re, h_im, ar, ai, bre, bim, cre, cim, d_skip.reshape(1, d), w_glu,
      b_glu.reshape(1, 2 * d))


def kernel(x_prompt, x_sample, cache_k, cache_v, state_ssm_re, state_ssm_im, page_table, rel_bias, norm_g, final_norm_g, ffn_w_in, ffn_w_out, att_w_qkv, att_w_o, ssm_log_dt, ssm_a_re, ssm_a_im, ssm_b_re, ssm_b_im, ssm_c_re, ssm_c_im, ssm_d, ssm_w_glu, ssm_b_glu):
    bp, t, d = x_prompt.shape
    bs, ts, _ = x_sample.shape
    assert ts == 1
    n_heads = cache_k.shape[3]
    hd = cache_k.shape[4]
    page = cache_k.shape[2]
    n_phys = cache_k.shape[1]
    n_pages = page_table.shape[1]
    past_len = n_pages * page
    assert MOBA_BLOCK % page == 0 and past_len % MOBA_BLOCK == 0
    ppb = MOBA_BLOCK // page
    n_full = past_len // MOBA_BLOCK
    assert n_full >= MOBA_TOPK
    n_groups, n_state = ssm_a_re.shape[1], ssm_a_re.shape[2]
    oct_g = LANES // ssm_b_re.shape[-1]
    n_oct = n_groups // oct_g
    depth = norm_g.shape[0]
    assert depth == 2

    w_in = ffn_w_in.astype(BF16)
    w_out = ffn_w_out.astype(BF16)
    w_qkv = att_w_qkv[0].astype(BF16)
    w_o = att_w_o[0].astype(BF16)
    w_glu = ssm_w_glu[0].astype(BF16)

    xp = x_prompt.reshape(bp * t, d)
    xs = x_sample.reshape(bs, d)

    xp = _ffn(xp, norm_g[0, 0], w_in, w_out, (0, 0), name="ffn_prompt")
    xs = _ffn(xs, norm_g[0, 0], w_in, w_out, (0, 0), name="ffn_sample")

    tiles, dec = _bias_tables(rel_bias, n_heads)

    q_s, k_s, v_s = _qkv_sample(xs, norm_g[0, 1], w_qkv)
    ckt = cache_k[0].transpose(0, 2, 3, 1)
    cvt = cache_v[0].transpose(0, 2, 3, 1)

    kt_p, vt_p, qb, kb, vb, km = _qkv_prompt(xp, norm_g[0, 1], w_qkv, float(hd) ** -0.5 * LOG2E, t)
    o_p, idx = _moba_prompt(qb.reshape(bp, t, d), kb.reshape(bp, t, d), vb.reshape(bp, t, d),
                            km.reshape(bp, t // MOBA_BLOCK, d), tiles, rel_bias, page_table, ckt, q_s,
                            n_heads=n_heads, n_full=n_full, ppb=ppb)
    xp = _ffn(xp, norm_g[0, 2], w_in, w_out, (0, 1), pre=(o_p.reshape(bp * t, d), w_o), name="ffn_prompt_o")

    idx_flat = idx[:, :, :MOBA_TOPK].reshape(-1)
    o_s = _decode_attn(page_table, idx_flat, ckt, cvt, q_s, k_s, v_s, dec, n_full=n_full, ppb=ppb)
    xs = _ffn(xs, norm_g[0, 2], w_in, w_out, (0, 1), pre=(o_s.reshape(bs, d).astype(BF16), w_o),
              name="ffn_sample_o")

    xp = _ffn(xp, norm_g[1, 0], w_in, w_out, (1, 0), name="ffn_prompt_l1")
    xs = _ffn(xs, norm_g[1, 0], w_in, w_out, (1, 0), name="ffn_sample_l1")
    prm = _s5_params(ssm_log_dt[0], ssm_a_re[0], ssm_a_im[0], ssm_b_re[0], ssm_b_im[0], ssm_c_re[0], ssm_c_im[0],
                     oct_g)
    xp3, hr_p, hi_p = _s5_prompt(xp.reshape(bp, t, d), norm_g[1, 1], prm, ssm_d[0], w_glu, ssm_b_glu[0], n_oct=n_oct)
    xs, hr_s, hi_s = _s5_sample(xs, norm_g[1, 1], state_ssm_re[0].reshape(bs, n_groups * n_state),
                                state_ssm_im[0].reshape(bs, n_groups * n_state), prm, ssm_d[0], w_glu,
                                ssm_b_glu[0], n_oct=n_oct)
    xp = _ffn(xp3.reshape(bp * t, d), norm_g[1, 2], w_in, w_out, (1, 1), final_g=final_norm_g,
              name="ffn_prompt_final")
    xs = _ffn(xs, norm_g[1, 2], w_in, w_out, (1, 1), final_g=final_norm_g, name="ffn_sample_final")

    st = lambda a, b: a.reshape(1, b, n_groups, n_state)
    tok_major = lambda a: a.reshape(1, bp, n_heads, hd, t).transpose(0, 1, 4, 2, 3)
    return (xp.reshape(bp, t, d), xs.reshape(bs, 1, d), tok_major(kt_p), tok_major(vt_p),
            k_s.reshape(1, bs, 1, n_heads, hd), v_s.reshape(1, bs, 1, n_heads, hd),
            st(hr_p, bp), st(hi_p, bp), st(hr_s, bs), st(hi_s, bs))
```

```python
import functools
import math

import numpy as np
import jax
import jax.numpy as jnp
from jax import lax
from jax.experimental import pallas as pl
from jax.experimental.pallas import tpu as pltpu

F32 = jnp.float32
BF16 = jnp.bfloat16
I32 = jnp.int32

RMS_EPS = 1e-6
NEG = -1e30
LOG2E = math.log2(math.e)
MOBA_BLOCK = 256
MOBA_TOPK = 3
REL_MAX_DIST = 128
LANES = 128
SUBLANES = 8
VMEM_CAP = 60 * 1024 * 1024

_NT = (((1,), (1,)), ((), ()))


def _vmem_limit(nbytes):
    return int(min(VMEM_CAP, nbytes + (8 << 20)))


def _const_spec(shape):
    nd = len(shape)
    return pl.BlockSpec(shape, lambda *_: (0,) * nd, pipeline_mode=pl.Buffered(1))


def _rms(x, g):
    return x * lax.rsqrt(jnp.mean(x * x, axis=-1, keepdims=True) + RMS_EPS) * g


def _split3(x):
    a = x.astype(BF16)
    r = x - a.astype(F32)
    b = r.astype(BF16)
    c = (r - b.astype(F32)).astype(BF16)
    return a, b, c


def _ffn_kernel(*refs, d_ff, chunk, has_pre, has_final):
    it = iter(refs)
    x_ref = next(it)
    if has_pre:
        a_ref, wpre_ref = next(it), next(it)
    g_ref, win_ref, wout_ref = next(it), next(it), next(it)
    if has_final:
        gf_ref = next(it)
    o_ref, act_ref = next(it), next(it)

    x = x_ref[...]
    if has_pre:
        x = x + jnp.dot(a_ref[...], wpre_ref[...], preferred_element_type=F32)
    xn = _rms(x, g_ref[...]).astype(BF16)
    for c in range(d_ff // chunk):
        lo = c * chunk
        gate = jnp.dot(xn, win_ref[:, lo:lo + chunk], preferred_element_type=F32)
        up = jnp.dot(xn, win_ref[:, d_ff + lo:d_ff + lo + chunk], preferred_element_type=F32)
        act_ref[:, lo:lo + chunk] = (gate * jax.nn.sigmoid(gate) * up).astype(BF16)
    y = x + 0.5 * jnp.dot(act_ref[...], wout_ref[...], preferred_element_type=F32)
    if has_final:
        y = _rms(y, gf_ref[...])
    o_ref[...] = y


def _ffn(x, g, w_in, w_out, wsel, *, pre=None, final_g=None, name):
    n, d = x.shape
    d_ff = w_out.shape[2]
    tm = min(n, 512)
    chunk = 256
    assert n % tm == 0 and d_ff % chunk == 0
    row = lambda i: (i, 0)
    picked = lambda w: pl.BlockSpec((None, None) + w.shape[2:], lambda *_: wsel + (0, 0),
                                    pipeline_mode=pl.Buffered(1))
    args, specs = [x], [pl.BlockSpec((tm, d), row)]
    if pre is not None:
        a, w_pre = pre
        args += [a, w_pre]
        specs += [pl.BlockSpec((tm, a.shape[1]), row), _const_spec(w_pre.shape)]
    args += [g.reshape(1, d), w_in, w_out]
    specs += [_const_spec((1, d)), picked(w_in), picked(w_out)]
    if final_g is not None:
        args.append(final_g.reshape(1, d))
        specs.append(_const_spec((1, d)))
    wbytes = 2 * (d * 2 * d_ff + d_ff * d + (pre[1].size if pre is not None else 0))
    need = wbytes + tm * d * 4 * 4 + tm * d_ff * 2 + tm * chunk * 4 * 6 + tm * d * 4 * 3
    return pl.pallas_call(
        functools.partial(_ffn_kernel, d_ff=d_ff, chunk=chunk, has_pre=pre is not None,
                          has_final=final_g is not None),
        grid=(n // tm,),
        in_specs=specs,
        out_specs=pl.BlockSpec((tm, d), row),
        out_shape=jax.ShapeDtypeStruct((n, d), F32),
        scratch_shapes=[pltpu.VMEM((tm, d_ff), BF16)],
        compiler_params=pltpu.CompilerParams(dimension_semantics=("parallel",),
                                             vmem_limit_bytes=_vmem_limit(need)),
        name=name,
    )(*args)


def _qkv_prompt_kernel(x_ref, g_ref, w_ref, kt_ref, vt_ref, qb_ref, kb_ref, vb_ref, km_ref, *, d, scale):
    xn = _rms(x_ref[...], g_ref[...]).astype(BF16)
    q = jnp.dot(xn, w_ref[:, 0:d], preferred_element_type=F32)
    qb_ref[...] = (q * scale).astype(BF16)
    k = jnp.dot(xn, w_ref[:, d:2 * d], preferred_element_type=F32)
    kt_ref[0] = k.T
    kb_ref[...] = k.astype(BF16)
    for r in range(k.shape[0] // MOBA_BLOCK):
        km_ref[r] = jnp.sum(k[r * MOBA_BLOCK:(r + 1) * MOBA_BLOCK], axis=0, keepdims=True) * (1.0 / MOBA_BLOCK)
    v = jnp.dot(xn, w_ref[:, 2 * d:3 * d], preferred_element_type=F32)
    vt_ref[0] = v.T
    vb_ref[...] = v.astype(BF16)


def _qkv_prompt(x, g, w_qkv, scale, t):
    n, d = x.shape
    tm = 512
    assert n % t == 0 and t % tm == 0 and tm % MOBA_BLOCK == 0
    tpb = t // tm
    row = lambda i: (i, 0)
    rs = pl.BlockSpec((tm, d), row)
    ts = pl.BlockSpec((1, d, tm), lambda i: (i // tpb, 0, i % tpb))
    need = 2 * w_qkv.size + tm * d * (4 * 2 * 3 + 2 * 2 * 3) + tm * d * 4 * 6
    return pl.pallas_call(
        functools.partial(_qkv_prompt_kernel, d=d, scale=scale),
        grid=(n // tm,),
        in_specs=[rs, _const_spec((1, d)), _const_spec(w_qkv.shape)],
        out_specs=[ts, ts, rs, rs, rs,
                   pl.BlockSpec((tm // MOBA_BLOCK, 1, d), lambda i: (i, 0, 0))],
        out_shape=[jax.ShapeDtypeStruct((n // t, d, t), F32), jax.ShapeDtypeStruct((n // t, d, t), F32),
                   jax.ShapeDtypeStruct((n, d), BF16), jax.ShapeDtypeStruct((n, d), BF16),
                   jax.ShapeDtypeStruct((n, d), BF16),
                   jax.ShapeDtypeStruct((n // MOBA_BLOCK, 1, d), F32)],
        compiler_params=pltpu.CompilerParams(dimension_semantics=("parallel",),
                                             vmem_limit_bytes=_vmem_limit(need)),
        name="qkv_prompt",
    )(x, g.reshape(1, d), w_qkv)


def _qkv_sample_kernel(x_ref, g_ref, w_ref, q_ref, k_ref, v_ref, *, d):
    xn = _rms(x_ref[...], g_ref[...]).astype(BF16)
    q_ref[...] = jnp.dot(xn, w_ref[:, 0:d], preferred_element_type=F32)
    k_ref[...] = jnp.dot(xn, w_ref[:, d:2 * d], preferred_element_type=F32)
    v_ref[...] = jnp.dot(xn, w_ref[:, 2 * d:3 * d], preferred_element_type=F32)


def _qkv_sample(x, g, w_qkv):
    n, d = x.shape
    full = pl.BlockSpec((n, d), lambda i: (0, 0))
    sd = jax.ShapeDtypeStruct((n, d), F32)
    return pl.pallas_call(
        functools.partial(_qkv_sample_kernel, d=d),
        grid=(1,),
        in_specs=[full, _const_spec((1, d)), _const_spec(w_qkv.shape)],
        out_specs=[full, full, full],
        out_shape=[sd, sd, sd],
        compiler_params=pltpu.CompilerParams(vmem_limit_bytes=_vmem_limit(2 * w_qkv.size + 16 * n * d)),
        name="qkv_sample",
    )(x, g.reshape(1, d), w_qkv)


def _rel_bucket_np(n, n_buckets):
    n = np.maximum(n, 0)
    max_exact = n_buckets // 2
    nf = np.maximum(n, max_exact).astype(np.float32)
    large = max_exact + (np.log(nf / np.float32(max_exact)) / np.float32(math.log(REL_MAX_DIST / max_exact))
                         * np.float32(n_buckets - max_exact)).astype(np.int32)
    large = np.minimum(large, n_buckets - 1)
    return np.where(n < max_exact, n, large).astype(np.int32)


def _bias_kernel(rel_ref, bkt_ref, dbk_ref, tile_ref, dec_ref, *, n_buckets):
    h = pl.program_id(0)
    for t in range(2):
        b = bkt_ref[t]
        acc = jnp.full(b.shape, NEG, F32)
        for u in range(n_buckets):
            acc = jnp.where(b == u, rel_ref[u, h], acc)
        tile_ref[0, t] = acc * LOG2E
    b = dbk_ref[...]
    acc = jnp.zeros(b.shape, F32)
    for u in range(n_buckets):
        acc = jnp.where(b == u, rel_ref[u, h], acc)
    dec_ref[0] = acc


def _bias_tables(rel_bias, n_heads):
    n_buckets = rel_bias.shape[0]
    blk = MOBA_BLOCK
    qi = np.arange(blk)[:, None]
    ki = np.arange(blk)[None, :]
    own = np.where(ki <= qi, _rel_bucket_np(qi - ki, n_buckets), -1)
    adj = _rel_bucket_np(blk + qi - ki, n_buckets)
    bkt = np.stack([own, adj]).astype(np.int32)
    dbk = np.zeros((SUBLANES, blk), np.int32)
    dbk[0] = _rel_bucket_np(blk - np.arange(blk), n_buckets)
    dbk[1] = _rel_bucket_np(np.full((blk,), 2 * blk), n_buckets)
    dbk[2] = 0
    return pl.pallas_call(
        functools.partial(_bias_kernel, n_buckets=n_buckets),
        grid=(n_heads,),
        in_specs=[pl.BlockSpec(memory_space=pltpu.SMEM),
                  pl.BlockSpec((2, blk, blk), lambda h: (0, 0, 0)),
                  pl.BlockSpec((SUBLANES, blk), lambda h: (0, 0))],
        out_specs=[pl.BlockSpec((1, 2, blk, blk), lambda h: (h, 0, 0, 0)),
                   pl.BlockSpec((1, SUBLANES, blk), lambda h: (h, 0, 0))],
        out_shape=[jax.ShapeDtypeStruct((n_heads, 2, blk, blk), F32),
                   jax.ShapeDtypeStruct((n_heads, SUBLANES, blk), F32)],
        name="rel_bias_tables",
    )(rel_bias, jnp.asarray(bkt), jnp.asarray(dbk))


def _moba_kernel(pt_ref, rel_ref, q_ref, k_ref, v_ref, km_ref, tile_ref, *rest, n_blk, hd, far_bucket, gate):
    n_gate_pages = gate["bps"] * gate["ppb"]
    gate_pages = rest[0:n_gate_pages]
    qs_ref, o_ref, idx_ref, q2_ref, s_ref, m_ref, acc_ref, qb_ref, g_ref = rest[n_gate_pages:]
    blk = MOBA_BLOCK
    half = 2 * blk
    rows = 2 * half
    dead = 2 * n_blk
    hp = pl.program_id(1)
    p = pl.program_id(2)
    lane = lax.broadcasted_iota(I32, (blk, LANES), 1)

    step = (pl.program_id(0) * pl.num_programs(1) + hp) * pl.num_programs(2) + p
    chunk = step % gate["chunks"]
    _decode_gate_step(gate_pages, qs_ref, qb_ref, g_ref, chunk, ppb=gate["ppb"], bps=gate["bps"],
                      n_heads=gate["n_heads"], hd=gate["hd"])

    def two_heads(q):
        return [jnp.where(lane < hd, q, jnp.zeros_like(q)), jnp.where(lane >= hd, q, jnp.zeros_like(q))]

    q2 = jnp.concatenate(two_heads(q_ref[0, 0:blk, :]) + two_heads(q_ref[0, blk:2 * blk, :]), axis=0)
    km = km_ref[0]
    km_hi = km.astype(BF16)
    km_lo = (km - km_hi.astype(F32)).astype(BF16)
    gt = (lax.dot_general(km_hi, q2, _NT, preferred_element_type=F32)
          + lax.dot_general(km_lo, q2, _NT, preferred_element_type=F32))
    bidx = lax.broadcasted_iota(I32, (n_blk, rows), 0)
    qcol = lax.broadcasted_iota(I32, (n_blk, rows), 1)
    i_col = 2 * p + jnp.where(qcol >= half, 1, 0)
    valid = bidx < i_col
    gm = jnp.where(valid, gt, NEG)
    rank = jnp.zeros((n_blk, rows), I32)
    for jp in range(n_blk):
        row = gm[jp:jp + 1, :]
        rank = rank + jnp.where(row > gm, 1, jnp.where(row == gm, jnp.where(bidx > jp, 1, 0), 0))
    far = jnp.where((qcol & blk) == 0, rel_ref[far_bucket, 2 * hp], rel_ref[far_bucket, 2 * hp + 1]) * LOG2E
    chosen = jnp.where(rank < MOBA_TOPK, jnp.where(bidx <= i_col - 2, far, 0.0), NEG)
    aval = jnp.where(valid, chosen, jnp.where(bidx == i_col, 0.0, NEG))
    a_hi = aval.astype(BF16).astype(F32)
    a_lo = (aval - a_hi).astype(BF16).astype(F32)
    dead_rows = jnp.where(lax.broadcasted_iota(I32, (SUBLANES, rows), 0) == 0, NEG, 0.0)
    at = jnp.concatenate([a_hi, a_lo, dead_rows, jnp.zeros((LANES - dead - SUBLANES, rows), F32)], axis=0)
    q2_ref[:, 0:LANES] = q2
    q2_ref[:, LANES:2 * LANES] = at.T.astype(BF16)

    ones_col = jnp.where(lane == 0, 1.0, 0.0).astype(BF16)
    n_far = jnp.maximum(2 * p - 1, 0)

    def scores(j, masked, r0=0):
        kj = k_ref[0, pl.ds(pl.multiple_of(j * blk, blk), blk), :]
        c_hi = jnp.where(masked, dead, j)
        c_lo = jnp.where(masked, dead, n_blk + j)
        ej = jnp.where((lane == c_hi) | (lane == c_lo), 1.0, 0.0).astype(BF16)
        return lax.dot_general(q2_ref[r0:rows], jnp.concatenate([kj, ej], axis=1), _NT, preferred_element_type=F32)

    def v_aug(j):
        vj = v_ref[0, pl.ds(pl.multiple_of(j * blk, blk), blk), :]
        return jnp.concatenate([vj, ones_col], axis=1)

    def lane_max(s):
        return jnp.maximum(s[:, 0:LANES], s[:, LANES:2 * LANES])

    def far_id(j):
        pad = j >= n_far
        return jnp.where(pad, n_blk - 1, j), pad

    t_own = jnp.concatenate([tile_ref[0, 0], tile_ref[1, 0]], axis=0)
    t_adj = jnp.concatenate([tile_ref[0, 1], tile_ref[1, 1]], axis=0)
    s0 = scores(jnp.maximum(2 * p - 1, 0), p == 0)
    s1 = scores(2 * p, False)
    s0a = s0[0:half] + t_adj
    s1a = s1[0:half] + t_own
    s1b = s1[half:rows] + t_adj
    s2b = scores(2 * p + 1, False, half) + t_own
    s_ref[0, 0:half] = s0a
    s_ref[0, half:rows] = s0[half:rows]
    s_ref[1, 0:half] = s1a
    s_ref[1, half:rows] = s1b
    s_ref[2, half:rows] = s2b
    m_ref[0:half] = jnp.maximum(lane_max(s0a), lane_max(s1a))
    m_ref[half:rows] = jnp.maximum(jnp.maximum(lane_max(s0[half:rows]), lane_max(s1b)), lane_max(s2b))

    n4 = (n_far + 1) // 4
    tail2 = n_far - 4 * n4 == 1

    def a_group(base, width):
        mm = m_ref[...]
        for u in range(width):
            j = base + u
            s = scores(*far_id(j))
            s_ref[3 + j] = s
            mm = jnp.maximum(mm, lane_max(s))
        m_ref[...] = mm

    def a_trip(it, c):
        a_group(it * 4, 4)
        return c

    lax.fori_loop(0, n4, a_trip, 0)

    @pl.when(tail2)
    def _():
        a_group(4 * n4, 2)

    m_ref[...] = jnp.broadcast_to(jnp.max(m_ref[...], axis=-1, keepdims=True), m_ref.shape)

    def prob(slot, r0=0):
        mb = m_ref[r0:rows]
        return jnp.exp2(s_ref[slot, r0:rows] - jnp.concatenate([mb, mb], axis=1)).astype(BF16)

    acc_ref[...] = (jnp.dot(prob(0), v_aug(jnp.maximum(2 * p - 1, 0)), preferred_element_type=F32)
                    + jnp.dot(prob(1), v_aug(2 * p), preferred_element_type=F32))
    acc_ref[half:rows] += jnp.dot(prob(2, half), v_aug(2 * p + 1), preferred_element_type=F32)

    def b_group(base, width):
        part = None
        for u in range(width):
            j = base + u
            d_ = jnp.dot(prob(3 + j), v_aug(far_id(j)[0]), preferred_element_type=F32)
            part = d_ if part is None else part + d_
        acc_ref[...] += part

    def b_trip(it, c):
        b_group(it * 4, 4)
        return c

    lax.fori_loop(0, n4, b_trip, 0)

    @pl.when(tail2)
    def _():
        b_group(4 * n4, 2)

    acc = acc_ref[...]
    for qb in range(2):
        r0 = qb * half
        o0 = acc[r0:r0 + blk, 0:LANES] / acc[r0:r0 + blk, LANES:LANES + 1]
        o1 = acc[r0 + blk:r0 + half, 0:LANES] / acc[r0 + blk:r0 + half, LANES:LANES + 1]
        o_ref[0, qb * blk:(qb + 1) * blk, :] = jnp.where(lane < hd, o0, o1).astype(BF16)

    _decode_gate_topk(idx_ref, g_ref, chunk, gate["chunks"], n_full=gate["n_full"])


def _moba_prompt(qb, kb, vb, km, tiles, rel_bias, page_table, cache_kt, q_s, *, n_heads, n_full, ppb):
    b, t, d = qb.shape
    hd = d // n_heads
    assert 2 * hd == LANES and t % MOBA_BLOCK == 0
    n_blk = t // MOBA_BLOCK
    assert 2 * n_blk + SUBLANES <= LANES and n_blk % 2 == 0
    blk = MOBA_BLOCK
    rows = 4 * blk
    far_bucket = int(_rel_bucket_np(np.array([2 * blk]), rel_bias.shape[0])[0])
    assert int(_rel_bucket_np(np.array([blk + 1]), rel_bias.shape[0])[0]) == far_bucket
    n_hp, n_p = n_heads // 2, n_blk // 2

    ns, ds = q_s.shape
    n_phys, heads_s, hd_s, page = cache_kt.shape
    assert page == LANES and n_full <= LANES
    n_pages = page_table.shape[1]
    n_steps = b * n_hp * n_p
    assert (ns * n_full) % n_steps == 0
    bps = ns * n_full // n_steps
    assert n_full % bps == 0
    chunks = n_full // bps
    gate = dict(n_full=n_full, ppb=ppb, bps=bps, n_heads=heads_s, hd=hd_s, chunks=chunks)

    def step_of(bi, hp, p):
        return (bi * n_hp + hp) * n_p + p

    def page_spec(bb, r):
        def imap(bi, hp, p, pt):
            st = step_of(bi, hp, p)
            return (pt[(st // chunks) * n_pages + ((st % chunks) * bps + bb) * ppb + r], 0, 0, 0)
        return pl.BlockSpec((1, heads_s, hd_s, page), imap)

    seq = pl.BlockSpec((1, t, LANES), lambda bi, hp, p, pt: (bi, 0, hp))
    qblk = pl.BlockSpec((1, 2 * blk, LANES), lambda bi, hp, p, pt: (bi, p, hp))
    per_seq = lambda width: pl.BlockSpec((1, width[0], width[1]),
                                         lambda bi, hp, p, pt: (step_of(bi, hp, p) // chunks, 0, 0))
    n_slots = 3 + n_blk
    need = (2 * 2 * t * LANES * 2 + n_slots * rows * blk * 4 + 2 * 4 * blk * blk * 4
            + rows * (2 * LANES * 2 + LANES * 4 + 2 * LANES * 4)
            + 2 * bps * ppb * ds * page * 4 + ds * LANES * 4 + (8 << 20))
    grid_spec = pltpu.PrefetchScalarGridSpec(
        num_scalar_prefetch=1,
        grid=(b, n_hp, n_p),
        in_specs=[pl.BlockSpec(memory_space=pltpu.SMEM), qblk, seq, seq,
                  pl.BlockSpec((1, n_blk, LANES), lambda bi, hp, p, pt: (bi, 0, hp)),
                  pl.BlockSpec((2, 2, blk, blk), lambda bi, hp, p, pt: (hp, 0, 0, 0))]
                 + [page_spec(bb, r) for bb in range(bps) for r in range(ppb)]
                 + [per_seq((1, ds))],
        out_specs=[qblk, per_seq((heads_s, LANES))],
        scratch_shapes=[pltpu.VMEM((rows, 2 * LANES), BF16), pltpu.VMEM((n_slots, rows, blk), F32),
                        pltpu.VMEM((rows, LANES), F32), pltpu.VMEM((rows, 2 * LANES), F32),
                        pltpu.VMEM((ds, LANES), F32), pltpu.VMEM((heads_s, LANES), F32)],
    )
    return pl.pallas_call(
        functools.partial(_moba_kernel, n_blk=n_blk, hd=hd, far_bucket=far_bucket, gate=gate),
        grid_spec=grid_spec,
        out_shape=[jax.ShapeDtypeStruct((b, t, d), BF16), jax.ShapeDtypeStruct((ns, heads_s, LANES), I32)],
        compiler_params=pltpu.CompilerParams(dimension_semantics=("arbitrary", "arbitrary", "arbitrary"),
                                             vmem_limit_bytes=_vmem_limit(need)),
        name="moba_prompt",
    )(page_table.reshape(-1), rel_bias, qb, kb, vb, km, tiles, *([cache_kt] * (bps * ppb)),
      q_s.reshape(ns, 1, ds))


def _decode_gate_step(pages, q_ref, qb_ref, g_ref, jj, *, ppb, bps, n_heads, hd):
    lane = lax.broadcasted_iota(I32, (n_heads, LANES), 1)

    @pl.when(jj == 0)
    def _():
        q128 = jnp.broadcast_to(q_ref[0], (LANES, n_heads * hd))
        for c in range(n_heads * hd // LANES):
            qb_ref[c * LANES:(c + 1) * LANES, :] = q128[:, c * LANES:(c + 1) * LANES].T
        g_ref[...] = jnp.zeros_like(g_ref)

    head = lax.broadcasted_iota(I32, (n_heads, LANES), 0)
    g = g_ref[...]
    for h in range(n_heads):
        accs = [None] * bps
        for c in range(hd // SUBLANES):
            rows = slice(c * SUBLANES, (c + 1) * SUBLANES)
            qv = qb_ref[h * hd + c * SUBLANES:h * hd + (c + 1) * SUBLANES, :]
            for bb in range(bps):
                kk = pages[bb * ppb][0, h, rows, :]
                for r in range(1, ppb):
                    kk = kk + pages[bb * ppb + r][0, h, rows, :]
                accs[bb] = kk * qv if accs[bb] is None else accs[bb] + kk * qv
        for bb in range(bps):
            tot = jnp.sum(jnp.sum(accs[bb], axis=0, keepdims=True), axis=-1, keepdims=True) * (1.0 / MOBA_BLOCK)
            g = jnp.where(lane == jj * bps + bb, jnp.where(head == h, tot, g), g)
    g_ref[...] = g


def _decode_gate_topk(idx_ref, g_ref, jj, n_chunks, *, n_full):
    n_heads = g_ref.shape[0]
    lane = lax.broadcasted_iota(I32, (n_heads, LANES), 1)

    @pl.when(jj == n_chunks - 1)
    def _():
        g = jnp.where(lane < n_full, g_ref[...], -3e38)
        out = jnp.zeros((n_heads, LANES), I32)
        for s in range(MOBA_TOPK):
            mx = jnp.max(g, axis=-1, keepdims=True)
            ix = jnp.min(jnp.where(g == mx, lane, LANES), axis=-1, keepdims=True)
            out = jnp.where(lane == s, ix, out)
            g = jnp.where(lane == ix, -3e38, g)
        idx_ref[0] = out


def _decode_attn_kernel(pt_ref, idx_ref, *refs, n_slots, ppb, n_full, hps, hd, scale):
    n_pg = hps * n_slots * ppb
    kp = refs[0:n_pg]
    vp = refs[n_pg:2 * n_pg]
    q_ref, kn_ref, vn_ref, dec_ref, o_ref = refs[2 * n_pg:]
    bg = pl.program_id(0) * pl.num_programs(1) + pl.program_id(1)
    page = kp[0].shape[-1]
    outs = []
    for hh in range(hps):
        cols = slice(hh * hd, (hh + 1) * hd)
        q = q_ref[0][:, cols]
        q8 = jnp.broadcast_to(q, (SUBLANES, hd)).astype(BF16)
        dec = dec_ref[hh]
        s_list, v_list = [], []
        for s in range(n_slots):
            blk_id = idx_ref[(bg * hps + hh) * n_slots + s]
            bias = jnp.where(blk_id == n_full - 1, dec[0:1, :], dec[1:2, :])
            for r in range(ppb):
                pg = (hh * n_slots + s) * ppb + r
                kt = kp[pg][0, 0].astype(BF16)
                sc = jnp.dot(q8, kt, preferred_element_type=F32) * scale + bias[:, r * page:(r + 1) * page]
                s_list.append(sc)
                v_list.append(vp[pg][0, 0].astype(BF16))
        s_self = jnp.sum(q * kn_ref[0][:, cols], axis=-1, keepdims=True) * scale + dec[2:3, 0:1]
        m = s_self
        for sc in s_list:
            m = jnp.maximum(m, jnp.max(sc, axis=-1, keepdims=True))
        p_self = jnp.exp(s_self - m)
        l = p_self
        acc = p_self * vn_ref[0][:, cols]
        for sc, vt in zip(s_list, v_list):
            p = jnp.exp(sc - m)
            l = l + jnp.sum(p, axis=-1, keepdims=True)
            acc = acc + lax.dot_general(p.astype(BF16), vt, _NT, preferred_element_type=F32)
        outs.append((acc / l)[0:1, :])
    o_ref[0] = jnp.concatenate(outs, axis=1)


def _decode_attn(page_table, idx_flat, cache_kt, cache_vt, q, k_new, v_new, dec, *, n_full, ppb):
    nb, d = q.shape
    n_phys, n_heads, hd, page = cache_kt.shape
    n_pages = page_table.shape[1]
    n_slots = MOBA_TOPK
    hps = 4
    assert n_heads % hps == 0
    n_grp = n_heads // hps

    def page_spec(hh, s, r):
        def imap(b, g, pt, idx):
            h = g * hps + hh
            blk_id = idx[(b * n_heads + h) * n_slots + s]
            return (pt[b * n_pages + blk_id * ppb + r], h, 0, 0)
        return pl.BlockSpec((1, 1, hd, page), imap)

    grp = pl.BlockSpec((1, 1, hps * hd), lambda b, g, pt, idx: (b * n_grp + g, 0, 0))
    pspecs = [page_spec(hh, s, r) for hh in range(hps) for s in range(n_slots) for r in range(ppb)]
    grid_spec = pltpu.PrefetchScalarGridSpec(
        num_scalar_prefetch=2,
        grid=(nb, n_grp),
        in_specs=pspecs + pspecs + [grp, grp, grp,
                                    pl.BlockSpec((hps, SUBLANES, MOBA_BLOCK), lambda b, g, pt, idx: (g, 0, 0))],
        out_specs=grp,
    )
    n_pg = hps * n_slots * ppb
    per_grp = lambda a: a.reshape(nb * n_grp, 1, hps * hd)
    return pl.pallas_call(
        functools.partial(_decode_attn_kernel, n_slots=n_slots, ppb=ppb, n_full=n_full, hps=hps, hd=hd,
                          scale=float(hd) ** -0.5),
        grid_spec=grid_spec,
        out_shape=jax.ShapeDtypeStruct((nb * n_grp, 1, hps * hd), F32),
        compiler_params=pltpu.CompilerParams(dimension_semantics=("arbitrary", "arbitrary")),
        name="decode_attn",
    )(page_table.reshape(-1), idx_flat, *([cache_kt] * n_pg), *([cache_vt] * n_pg),
      per_grp(q), per_grp(k_new), per_grp(v_new), dec)


def _s5_params(log_dt, a_re, a_im, b_re, b_im, c_re, c_im, oct_g):
    g, p = a_re.shape
    gd = b_re.shape[-1]
    dt = jnp.exp(log_dt.astype(F32))[:, None]
    ar, ai = a_re.astype(F32), a_im.astype(F32)
    mag = jnp.exp(dt * ar)
    abar_re, abar_im = mag * jnp.cos(dt * ai), mag * jnp.sin(dt * ai)
    den = ar * ar + ai * ai
    coef_re = ((abar_re - 1.0) * ar + abar_im * ai) / den
    coef_im = (abar_im * ar - (abar_re - 1.0) * ai) / den
    br, bim = b_re.astype(F32), b_im.astype(F32)
    bbar_re = coef_re[..., None] * br - coef_im[..., None] * bim
    bbar_im = coef_re[..., None] * bim + coef_im[..., None] * br
    n_oct = g // oct_g
    eye = jnp.eye(oct_g, dtype=F32)

    def in_blocks(bb):
        bb = bb.reshape(n_oct, oct_g, p, gd).transpose(0, 1, 3, 2)
        return jnp.einsum('ogip,gh->ogihp', bb, eye).reshape(n_oct, oct_g * gd, oct_g * p)

    def out_blocks(cc):
        cc = cc.reshape(n_oct, oct_g, gd, p).transpose(0, 1, 3, 2)
        return jnp.einsum('ogpi,gh->ogphi', cc, eye).reshape(n_oct, oct_g * p, oct_g * gd)

    return (abar_re.reshape(1, g * p), abar_im.reshape(1, g * p), in_blocks(bbar_re), in_blocks(bbar_im),
            out_blocks(c_re.astype(F32)), out_blocks(c_im.astype(F32)))


def _glu_tail(y, xn, d_ref, wg_ref, bg_ref, d):
    z = jax.nn.gelu(y + d_ref[...] * xn).astype(BF16)
    gl = jnp.dot(z, wg_ref[...], preferred_element_type=F32) + bg_ref[...]
    return gl[:, 0:d] * jax.nn.sigmoid(gl[:, d:2 * d])


def _s5_prompt_kernel(x_ref, g_ref, ar_ref, ai_ref, bre_ref, bim_ref, cre_ref, cim_ref, d_ref, wg_ref, bg_ref,
                      o_ref, hr_ref, hi_ref, xs_ref, un_ref, ub_ref, sre_ref, sim_ref, y_ref, *, n_oct, lc):
    nb, _, d = x_ref.shape
    rows = nb * lc
    ow = d // n_oct
    sw = ar_ref.shape[1] // n_oct

    @pl.when(pl.program_id(0) == 0)
    def _():
        hr_ref[...] = jnp.zeros_like(hr_ref)
        hi_ref[...] = jnp.zeros_like(hi_ref)

    xs_ref[...] = _rms(x_ref[...], g_ref[...])

    def to_time_major(t, c):
        un_ref[pl.ds(pl.multiple_of(t * nb, nb), nb), :] = xs_ref[:, pl.ds(t, 1), :].reshape(nb, d)
        return c

    lax.fori_loop(0, lc, to_time_major, 0, unroll=True)
    ub_ref[...] = un_ref[...].astype(BF16)

    for o in range(n_oct):
        uo = ub_ref[:, o * ow:(o + 1) * ow]
        sre_ref[...] = jnp.dot(uo, bre_ref[o], preferred_element_type=F32)
        sim_ref[...] = jnp.dot(uo, bim_ref[o], preferred_element_type=F32)
        ar = ar_ref[:, o * sw:(o + 1) * sw]
        ai = ai_ref[:, o * sw:(o + 1) * sw]

        def step(t, carry):
            hr, hi = carry
            r = pl.ds(pl.multiple_of(t * nb, nb), nb)
            nr = ar * hr - ai * hi + sre_ref[r, :]
            ni = ar * hi + ai * hr + sim_ref[r, :]
            sre_ref[r, :] = nr
            sim_ref[r, :] = ni
            return nr, ni

        hr, hi = lax.fori_loop(0, lc, step, (hr_ref[:, o * sw:(o + 1) * sw], hi_ref[:, o * sw:(o + 1) * sw]),
                               unroll=True)
        hr_ref[:, o * sw:(o + 1) * sw] = hr
        hi_ref[:, o * sw:(o + 1) * sw] = hi
        y_ref[:, o * ow:(o + 1) * ow] = (
            jnp.dot(sre_ref[...].astype(BF16), cre_ref[o], preferred_element_type=F32)
            - jnp.dot(sim_ref[...].astype(BF16), cim_ref[o], preferred_element_type=F32))

    rc = 128
    for r0 in range(0, rows, rc):
        y_ref[r0:r0 + rc, :] = _glu_tail(y_ref[r0:r0 + rc, :], un_ref[r0:r0 + rc, :], d_ref, wg_ref, bg_ref, d)

    def to_batch_major(t, c):
        mix = y_ref[pl.ds(pl.multiple_of(t * nb, nb), nb), :].reshape(nb, 1, d)
        o_ref[:, pl.ds(t, 1), :] = x_ref[:, pl.ds(t, 1), :] + mix
        return c

    lax.fori_loop(0, lc, to_batch_major, 0, unroll=True)


def _s5_prompt(x, g, prm, d_skip, w_glu, b_glu, *, n_oct):
    nb, t, d = x.shape
    assert nb == SUBLANES
    ar, ai, bre, bim, cre, cim = prm
    ns = ar.shape[1]
    lc = 64
    assert t % lc == 0
    rows = nb * lc
    ar8 = jnp.broadcast_to(ar, (nb, ns))
    ai8 = jnp.broadcast_to(ai, (nb, ns))
    xblk = pl.BlockSpec((nb, lc, d), lambda c: (0, c, 0))
    st = pl.BlockSpec((nb, ns), lambda c: (0, 0))
    need = (4 * nb * lc * d * 4 + 3 * rows * d * 4 + rows * d * 2 + 2 * rows * (ns // n_oct) * 4
            + 2 * (bre.size * 2 + cre.size * 2 + w_glu.size) + rows * 2 * d * 4 + 4 * nb * ns * 4)
    return pl.pallas_call(
        functools.partial(_s5_prompt_kernel, n_oct=n_oct, lc=lc),
        grid=(t // lc,),
        in_specs=[xblk, _const_spec((1, d)), _const_spec((nb, ns)), _const_spec((nb, ns)),
                  _const_spec(bre.shape), _const_spec(bim.shape), _const_spec(cre.shape), _const_spec(cim.shape),
                  _const_spec((1, d)), _const_spec(w_glu.shape), _const_spec((1, 2 * d))],
        out_specs=[xblk, st, st],
        out_shape=[jax.ShapeDtypeStruct((nb, t, d), F32), jax.ShapeDtypeStruct((nb, ns), F32),
                   jax.ShapeDtypeStruct((nb, ns), F32)],
        scratch_shapes=[pltpu.VMEM((nb, lc, d), F32), pltpu.VMEM((rows, d), F32), pltpu.VMEM((rows, d), BF16),
                        pltpu.VMEM((rows, ns // n_oct), F32), pltpu.VMEM((rows, ns // n_oct), F32),
                        pltpu.VMEM((rows, d), F32)],
        compiler_params=pltpu.CompilerParams(dimension_semantics=("arbitrary",),
                                             vmem_limit_bytes=_vmem_limit(need)),
        name="s5_prompt",
    )(x, g.reshape(1, d), ar8, ai8, bre.astype(BF16), bim.astype(BF16), cre.astype(BF16), cim.astype(BF16),
      d_skip.reshape(1, d), w_glu, b_glu.reshape(1, 2 * d))


def _s5_sample_kernel(x_ref, g_ref, h0r_ref, h0i_ref, ar_ref, ai_ref, bre_ref, bim_ref, cre_ref, cim_ref,
                      d_ref, wg_ref, bg_ref, o_ref, hr_ref, hi_ref, y_ref, *, n_oct):
    d = x_ref.shape[1]
    ow = d // n_oct
    sw = ar_ref.shape[1] // n_oct
    x = x_ref[...]
    xn = _rms(x, g_ref[...])
    u = _split3(xn)

    def dot3(lhs3, w):
        w1, w2, w3 = _split3(w)
        l1, l2, l3 = lhs3
        dd = lambda p, q_: jnp.dot(p, q_, preferred_element_type=F32)
        return ((dd(l1, w1) + dd(l1, w2)) + (dd(l2, w1) + dd(l1, w3))) + (dd(l2, w2) + dd(l3, w1))

    for o in range(n_oct):
        uo = tuple(p[:, o * ow:(o + 1) * ow] for p in u)
        ar = ar_ref[:, o * sw:(o + 1) * sw]
        ai = ai_ref[:, o * sw:(o + 1) * sw]
        hr = h0r_ref[:, o * sw:(o + 1) * sw]
        hi = h0i_ref[:, o * sw:(o + 1) * sw]
        nr = ar * hr - ai * hi + dot3(uo, bre_ref[o])
        ni = ar * hi + ai * hr + dot3(uo, bim_ref[o])
        hr_ref[:, o * sw:(o + 1) * sw] = nr
        hi_ref[:, o * sw:(o + 1) * sw] = ni
        y_ref[:, o * ow:(o + 1) * ow] = (
            jnp.dot(nr.astype(BF16), cre_ref[o].astype(BF16), preferred_element_type=F32)
            - jnp.dot(ni.astype(BF16), cim_ref[o].astype(BF16), preferred_element_type=F32))
    o_ref[...] = x + _glu_tail(y_ref[...], xn, d_ref, wg_ref, bg_ref, d)


def _s5_sample(x, g, h_re, h_im, prm, d_skip, w_glu, b_glu, *, n_oct):
    n, d = x.shape
    ar, ai, bre, bim, cre, cim = prm
    ns = ar.shape[1]
    full = lambda shape: pl.BlockSpec(shape, lambda i: (0,) * len(shape))
    need = 4 * (bre.size * 2 + cre.size * 2) + 2 * w_glu.size + 16 * n * ns * 4
    return pl.pallas_call(
        functools.partial(_s5_sample_kernel, n_oct=n_oct),
        grid=(1,),
        in_specs=[full((n, d)), full((1, d)), full((n, ns)), full((n, ns)), full((1, ns)), full((1, ns)),
                  _const_spec(bre.shape), _const_spec(bim.shape), _const_spec(cre.shape), _const_spec(cim.shape),
                  full((1, d)), _const_spec(w_glu.shape), full((1, 2 * d))],
        out_specs=[full((n, d)), full((n, ns)), full((n, ns))],
        out_shape=[jax.ShapeDtypeStruct((n, d), F32), jax.ShapeDtypeStruct((n, ns), F32),
                   jax.ShapeDtypeStruct((n, ns), F32)],
        scratch_shapes=[pltpu.VMEM((n, d), F32)],
        compiler_params=pltpu.CompilerParams(vmem_limit_bytes=_vmem_limit(need)),
        name="s5_sample",
    )(x, g.reshape(1, d), h_re, h_im, ar, ai, bre, bim, cre, cim, d_skip.reshape(1, d), w_glu,
      b_glu.reshape(1, 2 * d))


def kernel(x_prompt, x_sample, cache_k, cache_v, state_ssm_re, state_ssm_im, page_table, rel_bias, norm_g, final_norm_g, ffn_w_in, ffn_w_out, att_w_qkv, att_w_o, ssm_log_dt, ssm_a_re, ssm_a_im, ssm_b_re, ssm_b_im, ssm_c_re, ssm_c_im, ssm_d, ssm_w_glu, ssm_b_glu):
    bp, t, d = x_prompt.shape
    bs, ts, _ = x_sample.shape
    assert ts == 1
    n_heads = cache_k.shape[3]
    hd = cache_k.shape[4]
    page = cache_k.shape[2]
    n_phys = cache_k.shape[1]
    n_pages = page_table.shape[1]
    past_len = n_pages * page
    assert MOBA_BLOCK % page == 0 and past_len % MOBA_BLOCK == 0
    ppb = MOBA_BLOCK // page
    n_full = past_len // MOBA_BLOCK
    assert n_full >= MOBA_TOPK
    n_groups, n_state = ssm_a_re.shape[1], ssm_a_re.shape[2]
    oct_g = LANES // ssm_b_re.shape[-1]
    n_oct = n_groups // oct_g
    depth = norm_g.shape[0]
    assert depth == 2

    w_in = ffn_w_in.astype(BF16)
    w_out = ffn_w_out.astype(BF16)
    w_qkv = att_w_qkv[0].astype(BF16)
    w_o = att_w_o[0].astype(BF16)
    w_glu = ssm_w_glu[0].astype(BF16)

    xp = x_prompt.reshape(bp * t, d)
    xs = x_sample.reshape(bs, d)

    xp = _ffn(xp, norm_g[0, 0], w_in, w_out, (0, 0), name="ffn_prompt")
    xs = _ffn(xs, norm_g[0, 0], w_in, w_out, (0, 0), name="ffn_sample")

    tiles, dec = _bias_tables(rel_bias, n_heads)

    q_s, k_s, v_s = _qkv_sample(xs, norm_g[0, 1], w_qkv)
    ckt = cache_k[0].transpose(0, 2, 3, 1)
    cvt = cache_v[0].transpose(0, 2, 3, 1)

    kt_p, vt_p, qb, kb, vb, km = _qkv_prompt(xp, norm_g[0, 1], w_qkv, float(hd) ** -0.5 * LOG2E, t)
    o_p, idx = _moba_prompt(qb.reshape(bp, t, d), kb.reshape(bp, t, d), vb.reshape(bp, t, d),
                            km.reshape(bp, t // MOBA_BLOCK, d), tiles, rel_bias, page_table, ckt, q_s,
                            n_heads=n_heads, n_full=n_full, ppb=ppb)
    xp = _ffn(xp, norm_g[0, 2], w_in, w_out, (0, 1), pre=(o_p.reshape(bp * t, d), w_o), name="ffn_prompt_o")

    idx_flat = idx[:, :, :MOBA_TOPK].reshape(-1)
    o_s = _decode_attn(page_table, idx_flat, ckt, cvt, q_s, k_s, v_s, dec, n_full=n_full, ppb=ppb)
    xs = _ffn(xs, norm_g[0, 2], w_in, w_out, (0, 1), pre=(o_s.reshape(bs, d).astype(BF16), w_o),
              name="ffn_sample_o")

    xp = _ffn(xp, norm_g[1, 0], w_in, w_out, (1, 0), name="ffn_prompt_l1")
    xs = _ffn(xs, norm_g[1, 0], w_in, w_out, (1, 0), name="ffn_sample_l1")
    prm = _s5_params(ssm_log_dt[0], ssm_a_re[0], ssm_a_im[0], ssm_b_re[0], ssm_b_im[0], ssm_c_re[0], ssm_c_im[0],
                     oct_g)
    xp3, hr_p, hi_p = _s5_prompt(xp.reshape(bp, t, d), norm_g[1, 1], prm, ssm_d[0], w_glu, ssm_b_glu[0], n_oct=n_oct)
    xs, hr_s, hi_s = _s5_sample(xs, norm_g[1, 1], state_ssm_re[0].reshape(bs, n_groups * n_state),
                                state_ssm_im[0].reshape(bs, n_groups * n_state), prm, ssm_d[0], w_glu,
                                ssm_b_glu[0], n_oct=n_oct)
    xp = _ffn(xp3.reshape(bp * t, d), norm_g[1, 2], w_in, w_out, (1, 1), final_g=final_norm_g,
              name="ffn_prompt_final")
    xs = _ffn(xs, norm_g[1, 2], w_in, w_out, (1, 1), final_g=final_norm_g, name="ffn_sample_final")

    st = lambda a, b: a.reshape(1, b, n_groups, n_state)
    tok_major = lambda a: a.reshape(1, bp, n_heads, hd, t).transpose(0, 1, 4, 2, 3)
    return (xp.reshape(bp, t, d), xs.reshape(bs, 1, d), tok_major(kt_p), tok_major(vt_p),
            k_s.reshape(1, bs, 1, n_heads, hd), v_s.reshape(1, bs, 1, n_heads, hd),
            st(hr_p, bp), st(hi_p, bp), st(hr_s, bs), st(hi_s, bs))
```
